```python
import jax, jax.numpy as jnp
from jax import lax
import numpy as np

D_MODEL = 1024
BATCH = 4
SEQ = 4096
DEPTH = 1

CHUNK = 64
N_FOX_HEADS = 8
N_SB_HEADS = 8
HEAD_DIM = D_MODEL // (N_FOX_HEADS + N_SB_HEADS)
FOX_WIDTH = N_FOX_HEADS * HEAD_DIM
SB_WIDTH = N_SB_HEADS * HEAD_DIM
MIX_WIDTH = FOX_WIDTH + SB_WIDTH
PROJ_DIM = 3 * FOX_WIDTH + N_FOX_HEADS + 3 * SB_WIDTH
Q_BLOCK = 128
N_EXPERTS = 32
TOP_K = 4
D_FF = D_MODEL
SWIGLU_LIMIT = 7.0
SWIGLU_ALPHA = 1.702
MOE_BLOCK = 128
RMS_EPS = 1e-5

kernel_name = "hymba_fox_stickbreaking_moe_block"


def _rmsnorm(x, g):
    xf = x.astype(jnp.float32)
    y = xf * lax.rsqrt(jnp.mean(xf * xf, axis=-1, keepdims=True) + RMS_EPS)
    return (y * g.astype(jnp.float32)).astype(x.dtype)


def _to_heads(t, n_heads):
    b, s, _ = t.shape
    return t.reshape(b, s, n_heads, HEAD_DIM).transpose(0, 2, 1, 3).astype(jnp.float32)


def _from_heads(t):
    b, h, s, d = t.shape
    return t.transpose(0, 2, 1, 3).reshape(b, s, h * d)


def _forgetting_attention(q, k, v, c):
    seq = q.shape[2]
    scale = HEAD_DIM ** -0.5
    outs = []
    for i in range(seq // Q_BLOCK):
        q0, q1 = i * Q_BLOCK, (i + 1) * Q_BLOCK
        s = jnp.einsum('bhqd,bhkd->bhqk', q[:, :, q0:q1], k[:, :, :q1]) * scale
        s = s + c[:, :, q0:q1, None] - c[:, :, None, :q1]
        causal = jnp.arange(q0, q1)[:, None] >= jnp.arange(q1)[None, :]
        p = jax.nn.softmax(jnp.where(causal, s, -jnp.inf), axis=-1)
        outs.append(jnp.einsum('bhqk,bhkd->bhqd', p, v[:, :, :q1]))
    return jnp.concatenate(outs, axis=2)


def _stick_breaking_attention(q, k, v):
    seq = q.shape[2]
    scale = HEAD_DIM ** -0.5
    outs = []
    for i in range(seq // Q_BLOCK):
        q0, q1 = i * Q_BLOCK, (i + 1) * Q_BLOCK
        z = jnp.einsum('bhqd,bhkd->bhqk', q[:, :, q0:q1], k[:, :, :q1]) * scale
        strict = jnp.arange(q0, q1)[:, None] > jnp.arange(q1)[None, :]
        log_1mb = jnp.where(strict, jax.nn.log_sigmoid(-z), 0.0)
        suffix = lax.cumsum(log_1mb, axis=3, reverse=True) - log_1mb
        log_a = jnp.where(strict, jax.nn.log_sigmoid(z) + suffix, -jnp.inf)
        outs.append(jnp.einsum('bhqk,bhkd->bhqd', jnp.exp(log_a), v[:, :, :q1]))
    return jnp.concatenate(outs, axis=2)


def _moe(h, w_router, b_router, w1, b1, w2, b2):
    b, s, d = h.shape
    n_tok = b * s
    t = h.reshape(n_tok, d)
    logits = (t @ w_router + b_router).astype(jnp.float32)
    top_v, top_i = lax.top_k(logits, TOP_K)
    top_w = jax.nn.softmax(top_v, axis=-1)
    n_asg = n_tok * TOP_K
    flat_e = top_i.reshape(n_asg).astype(jnp.int32)
    flat_tok = jnp.arange(n_asg, dtype=jnp.int32) // TOP_K
    flat_w = top_w.reshape(n_asg)
    order = jnp.argsort(flat_e)
    se, stok, sw = flat_e[order], flat_tok[order], flat_w[order]
    counts = jnp.zeros((N_EXPERTS,), jnp.int32).at[flat_e].add(1)
    start = jnp.cumsum(counts) - counts
    padded = (counts + MOE_BLOCK - 1) // MOE_BLOCK * MOE_BLOCK
    pend = jnp.cumsum(padded)
    pstart = pend - padded
    dest = pstart[se] + jnp.arange(n_asg, dtype=jnp.int32) - start[se]
    n_blocks = -(-n_asg // MOE_BLOCK) + N_EXPERTS
    n_rows = n_blocks * MOE_BLOCK
    tok_pad = jnp.full((n_rows,), n_tok, jnp.int32).at[dest].set(stok)
    w_pad = jnp.zeros((n_rows,), jnp.float32).at[dest].set(sw)
    block_e = jnp.minimum(
        jnp.searchsorted(pend, jnp.arange(n_blocks, dtype=jnp.int32) * MOE_BLOCK, side='right'),
        N_EXPERTS - 1).astype(jnp.int32)
    t_pad = jnp.concatenate([t, jnp.zeros((1, d), t.dtype)], axis=0)
    xb = t_pad[tok_pad].reshape(n_blocks, MOE_BLOCK, d)

    def expert_rows(args):
        xe, e = args
        hh = xe @ w1[e] + b1[e]
        glu = jnp.minimum(hh[:, :D_FF], SWIGLU_LIMIT)
        lin = jnp.clip(hh[:, D_FF:], -SWIGLU_LIMIT, SWIGLU_LIMIT)
        act = glu * jax.nn.sigmoid(SWIGLU_ALPHA * glu) * (lin + 1.0)
        return act @ w2[e] + b2[e]

    yb = lax.map(expert_rows, (xb, block_e)).reshape(n_rows, d)
    out = jnp.zeros((n_tok + 1, d), jnp.float32).at[tok_pad].add(
        yb.astype(jnp.float32) * w_pad[:, None])
    return out[:n_tok].reshape(b, s, d).astype(h.dtype)


def setup_inputs(seed: int = 0) -> dict:
    key = jax.random.key(seed)
    ks = jax.random.split(key, 16)
    nrm = jax.random.normal
    L = DEPTH
    x = nrm(ks[0], (BATCH, SEQ, D_MODEL), jnp.float32)
    norm1_g = 1.0 + 0.02 * nrm(ks[1], (L, D_MODEL), jnp.float32)
    w_in = nrm(ks[2], (L, D_MODEL, PROJ_DIM), jnp.float32) * D_MODEL ** -0.5
    b_f = jnp.linspace(1.0, 5.0, N_FOX_HEADS, dtype=jnp.float32)[None, :] + 0.1 * nrm(ks[3], (L, N_FOX_HEADS), jnp.float32)
    g_fox = 1.0 + 0.02 * nrm(ks[4], (L, FOX_WIDTH), jnp.float32)
    g_sb = 1.0 + 0.02 * nrm(ks[5], (L, SB_WIDTH), jnp.float32)
    w_out = nrm(ks[6], (L, MIX_WIDTH, D_MODEL), jnp.float32) * MIX_WIDTH ** -0.5
    norm2_g = 1.0 + 0.02 * nrm(ks[7], (L, D_MODEL), jnp.float32)
    w_router = nrm(ks[8], (L, D_MODEL, N_EXPERTS), jnp.float32) * D_MODEL ** -0.5
    b_router = 0.01 * nrm(ks[9], (L, N_EXPERTS), jnp.float32)
    w1 = nrm(ks[10], (L, N_EXPERTS, D_MODEL, 2 * D_FF), jnp.float32) * D_MODEL ** -0.5
    b1 = 0.01 * nrm(ks[11], (L, N_EXPERTS, 2 * D_FF), jnp.float32)
    w2 = nrm(ks[12], (L, N_EXPERTS, D_FF, D_MODEL), jnp.float32) * D_FF ** -0.5
    b2 = 0.01 * nrm(ks[13], (L, N_EXPERTS, D_MODEL), jnp.float32)
    norm_f_g = 1.0 + 0.02 * nrm(ks[14], (D_MODEL,), jnp.float32)
    return {"x": x, "norm1_g": norm1_g, "w_in": w_in, "b_f": b_f, "g_fox": g_fox,
            "g_sb": g_sb, "w_out": w_out, "norm2_g": norm2_g, "w_router": w_router,
            "b_router": b_router, "w1": w1, "b1": b1, "w2": w2, "b2": b2,
            "norm_f_g": norm_f_g}


def reference(x, norm1_g, w_in, b_f, g_fox, g_sb, w_out, norm2_g, w_router,
              b_router, w1, b1, w2, b2, norm_f_g):
    b, s, _ = x.shape
    splits = [FOX_WIDTH, 2 * FOX_WIDTH, 3 * FOX_WIDTH, 3 * FOX_WIDTH + N_FOX_HEADS,
              3 * FOX_WIDTH + N_FOX_HEADS + SB_WIDTH,
              3 * FOX_WIDTH + N_FOX_HEADS + 2 * SB_WIDTH]
    for l in range(DEPTH):
        h = _rmsnorm(x, norm1_g[l])
        proj = h @ w_in[l]
        fq, fk, fv, f_logit, sq, sk, sv = jnp.split(proj, splits, axis=-1)
        log_f = jax.nn.log_sigmoid(f_logit.astype(jnp.float32) + b_f[l].astype(jnp.float32))
        c = jnp.cumsum(log_f, axis=1).transpose(0, 2, 1)
        fox = _forgetting_attention(_to_heads(fq, N_FOX_HEADS), _to_heads(fk, N_FOX_HEADS),
                                    _to_heads(fv, N_FOX_HEADS), c)
        sb = _stick_breaking_attention(_to_heads(sq, N_SB_HEADS), _to_heads(sk, N_SB_HEADS),
                                       _to_heads(sv, N_SB_HEADS))
        mix = jnp.concatenate([_rmsnorm(_from_heads(fox), g_fox[l]),
                               _rmsnorm(_from_heads(sb), g_sb[l])], axis=-1).astype(x.dtype)
        x = x + mix @ w_out[l]
        x = x + _moe(_rmsnorm(x, norm2_g[l]), w_router[l], b_router[l],
                     w1[l], b1[l], w2[l], b2[l])
    return _rmsnorm(x, norm_f_g)
```

```python
import functools

import jax
import jax.numpy as jnp
from jax import lax
from jax.experimental import pallas as pl
from jax.experimental.pallas import tpu as pltpu

HEAD_DIM = 64
N_FOX_HEADS = 8
N_SB_HEADS = 8
FOX_WIDTH = N_FOX_HEADS * HEAD_DIM
SB_WIDTH = N_SB_HEADS * HEAD_DIM
N_EXPERTS = 32
TOP_K = 4
SWIGLU_LIMIT = 7.0
SWIGLU_ALPHA = 1.702
RMS_EPS = 1e-5

LANES = 128
PAIRS = FOX_WIDTH // LANES

ROW_TILE = 256
Q_TILE = 256
K_TILE = 256
EXPERT_TILE = 256
DISPATCH_TILE = 512
COMBINE_TILE = 256
VMEM_LIMIT = 56 * 1024 * 1024

F32 = jnp.float32
BF16 = jnp.bfloat16


def _dot(a, b):
    return jnp.dot(a, b, preferred_element_type=F32)


def _dot_nt(a, b):
    return lax.dot_general(a, b, (((1,), (1,)), ((), ())), preferred_element_type=F32)


def _split2(v):
    hi = v.astype(BF16)
    lo = (v - hi.astype(F32)).astype(BF16)
    return hi, lo


def _split3(v):
    hi = v.astype(BF16)
    r = v - hi.astype(F32)
    mid = r.astype(BF16)
    lo = (r - mid.astype(F32)).astype(BF16)
    return hi, mid, lo


def _softplus(z):
    return jnp.maximum(z, 0.0) + jnp.log1p(jnp.exp(-jnp.abs(z)))


def _rms(v, g):
    return v * lax.rsqrt(jnp.mean(v * v, axis=-1, keepdims=True) + RMS_EPS) * g


def _in_proj_kernel(x_ref, g_ref, w_ref, wf_ref, bf_ref, proj_ref, c_ref, carry_ref):
    @pl.when(pl.program_id(1) == 0)
    def _():
        carry_ref[...] = jnp.zeros_like(carry_ref)

    tm = x_ref.shape[1]
    hb = _rms(x_ref[0], g_ref[...]).astype(BF16)
    proj_ref[0] = _dot(hb, w_ref[...]).astype(BF16)
    log_f = -_softplus(-(_dot_nt(wf_ref[...], hb) + bf_ref[...]))
    row = lax.broadcasted_iota(jnp.int32, (tm, tm), 0)
    col = lax.broadcasted_iota(jnp.int32, (tm, tm), 1)
    tri = (row <= col).astype(BF16)
    hi, mid, lo = _split3(log_f)
    c = _dot(hi, tri) + _dot(mid, tri) + _dot(lo, tri) + carry_ref[...]
    c_ref[0] = c
    carry_ref[...] = c[:, tm - 1:tm]


def _in_proj(x, g, w, wf_t, bf):
    b, s, d = x.shape
    pw = w.shape[1]
    tm = min(ROW_TILE, s)
    return pl.pallas_call(
        _in_proj_kernel,
        name="in_proj",
        grid=(b, s // tm),
        in_specs=[
            pl.BlockSpec((1, tm, d), lambda i, j: (i, j, 0)),
            pl.BlockSpec((1, d), lambda i, j: (0, 0)),
            pl.BlockSpec((d, pw), lambda i, j: (0, 0)),
            pl.BlockSpec((N_FOX_HEADS, d), lambda i, j: (0, 0)),
            pl.BlockSpec((N_FOX_HEADS, 1), lambda i, j: (0, 0)),
        ],
        out_specs=[
            pl.BlockSpec((1, tm, pw), lambda i, j: (i, j, 0)),
            pl.BlockSpec((1, N_FOX_HEADS, tm), lambda i, j: (i, 0, j)),
        ],
        out_shape=[
            jax.ShapeDtypeStruct((b, s, pw), BF16),
            jax.ShapeDtypeStruct((b, N_FOX_HEADS, s), F32),
        ],
        scratch_shapes=[pltpu.VMEM((N_FOX_HEADS, 1), F32)],
        compiler_params=pltpu.CompilerParams(
            dimension_semantics=("parallel", "arbitrary"),
            vmem_limit_bytes=VMEM_LIMIT),
    )(x, g, w, wf_t, bf)


def _fox_kernel(q_ref, k_ref, v_ref, c_ref, o_ref):
    p = pl.program_id(1)
    i = pl.program_id(2)
    tq = q_ref.shape[1]
    tk = K_TILE if K_TILE <= tq else tq
    q = q_ref[0]
    lane_q = lax.broadcasted_iota(jnp.int32, (tq, LANES), 1)
    lane_k = lax.broadcasted_iota(jnp.int32, (tk, LANES), 1)
    r_idx = lax.broadcasted_iota(jnp.int32, (tq, tk), 0)
    c_idx = lax.broadcasted_iota(jnp.int32, (tq, tk), 1)
    nkb = tq // tk
    outs = []
    for hh in range(2):
        mine_q = (lane_q < HEAD_DIM) if hh == 0 else (lane_q >= HEAD_DIM)
        mine_k = (lane_k < HEAD_DIM) if hh == 0 else (lane_k >= HEAD_DIM)
        qm = jnp.where(mine_q, q, jnp.zeros_like(q))
        head = 2 * p + hh
        c0 = c_ref[0, pl.ds(head, 1), pl.ds(pl.multiple_of(i * tq, tq), tk)][:, 0:1]

        def block(j, carry, masked):
            m, acc = carry
            k0 = pl.multiple_of(j * tk, tk)
            kb = k_ref[0, pl.ds(k0, tk), :]
            vb = v_ref[0, pl.ds(k0, tk), :]
            va = jnp.where(mine_k, vb, jnp.ones_like(vb))
            bias = c0 - c_ref[0, pl.ds(head, 1), pl.ds(k0, tk)]
            s = _dot_nt(qm, kb) + bias
            if masked is not None:
                s = jnp.where(masked, s, -jnp.inf)
            m_new = jnp.maximum(m, jnp.max(s, axis=-1, keepdims=True))
            pr = jnp.exp(s - m_new)
            acc = acc * jnp.exp(m - m_new) + _dot(pr.astype(BF16), va)
            return m_new, acc

        carry = (jnp.full((tq, 1), -1e30, F32), jnp.zeros((tq, LANES), F32))
        carry = lax.fori_loop(0, i * nkb, lambda j, cr: block(j, cr, None), carry)
        for jj in range(nkb):
            masked = (c_idx + jj * tk) <= r_idx
            carry = block(i * nkb + jj, carry, masked)
        _, acc = carry
        l_col = HEAD_DIM if hh == 0 else 0
        outs.append(acc / acc[:, l_col:l_col + 1])
    o_ref[0] = jnp.where(lane_q < HEAD_DIM, outs[0], outs[1])


def _fox(proj, c):
    b, s, _ = proj.shape
    tq = min(Q_TILE, s)
    return pl.pallas_call(
        _fox_kernel,
        name="fox",
        grid=(b, PAIRS, s // tq),
        in_specs=[
            pl.BlockSpec((1, tq, LANES), lambda bi, p, i: (bi, i, p)),
            pl.BlockSpec((1, s, LANES), lambda bi, p, i: (bi, 0, PAIRS + p)),
            pl.BlockSpec((1, s, LANES), lambda bi, p, i: (bi, 0, 2 * PAIRS + p)),
            pl.BlockSpec((1, N_FOX_HEADS, s), lambda bi, p, i: (bi, 0, 0)),
        ],
        out_specs=pl.BlockSpec((1, tq, LANES), lambda bi, p, i: (bi, i, p)),
        out_shape=jax.ShapeDtypeStruct((b, s, FOX_WIDTH), F32),
        compiler_params=pltpu.CompilerParams(
            dimension_semantics=("parallel", "parallel", "parallel"),
            vmem_limit_bytes=VMEM_LIMIT),
    )(proj, proj, proj, c)


def _stick_kernel(q_ref, k_ref, v_ref, o_ref):
    i = pl.program_id(2)
    tq = q_ref.shape[1]
    tk = K_TILE if K_TILE <= tq else tq
    q = q_ref[0]
    lane_q = lax.broadcasted_iota(jnp.int32, (tq, LANES), 1)
    r_idx = lax.broadcasted_iota(jnp.int32, (tq, tk), 0)
    c_idx = lax.broadcasted_iota(jnp.int32, (tq, tk), 1)
    kr = lax.broadcasted_iota(jnp.int32, (tk, tk), 0)
    kc = lax.broadcasted_iota(jnp.int32, (tk, tk), 1)
    later = (kr > kc).astype(BF16)
    nkb = tq // tk
    outs = []
    for hh in range(2):
        mine_q = (lane_q < HEAD_DIM) if hh == 0 else (lane_q >= HEAD_DIM)
        qm = jnp.where(mine_q, q, jnp.zeros_like(q))

        def block(j, carry, strict):
            tail, acc = carry
            k0 = pl.multiple_of(j * tk, tk)
            kb = k_ref[0, pl.ds(k0, tk), :]
            vb = v_ref[0, pl.ds(k0, tk), :]
            z = _dot_nt(qm, kb)
            sp = _softplus(z)
            if strict is not None:
                sp_m = jnp.where(strict, sp, 0.0)
            else:
                sp_m = sp
            hi, lo = _split2(sp_m)
            suffix = _dot(hi, later) + _dot(lo, later)
            log_a = (z - sp) - (suffix + tail)
            a = jnp.exp(log_a)
            if strict is not None:
                a = jnp.where(strict, a, 0.0)
            acc = acc + _dot(a.astype(BF16), vb)
            tail = tail + suffix[:, 0:1] + sp_m[:, 0:1]
            return tail, acc

        carry = (jnp.zeros((tq, 1), F32), jnp.zeros((tq, LANES), F32))
        for jj in reversed(range(nkb)):
            strict = (c_idx + jj * tk) < r_idx
            carry = block(i * nkb + jj, carry, strict)
        n_full = i * nkb
        carry = lax.fori_loop(
            0, n_full, lambda t, cr: block(n_full - 1 - t, cr, None), carry)
        outs.append(carry[1])
    o_ref[0] = jnp.where(lane_q < HEAD_DIM, outs[0], outs[1])


def _stick(proj):
    b, s, _ = proj.shape
    tq = min(Q_TILE, s)
    return pl.pallas_call(
        _stick_kernel,
        name="stick",
        grid=(b, PAIRS, s // tq),
        in_specs=[
            pl.BlockSpec((1, tq, LANES), lambda bi, p, i: (bi, i, 3 * PAIRS + p)),
            pl.BlockSpec((1, s, LANES), lambda bi, p, i: (bi, 0, 4 * PAIRS + p)),
            pl.BlockSpec((1, s, LANES), lambda bi, p, i: (bi, 0, 5 * PAIRS + p)),
        ],
        out_specs=pl.BlockSpec((1, tq, LANES), lambda bi, p, i: (bi, i, p)),
        out_shape=jax.ShapeDtypeStruct((b, s, SB_WIDTH), F32),
        compiler_params=pltpu.CompilerParams(
            dimension_semantics=("parallel", "parallel", "parallel"),
            vmem_limit_bytes=VMEM_LIMIT),
    )(proj, proj, proj)


def _out_proj_kernel(x_ref, fox_ref, sb_ref, gfox_ref, gsb_ref, wo_ref, g2_ref,
                     wr_ref, br_ref,
                     x1_ref, h2_ref, e_ref, w_ref, rank_ref, cnt_ref, carry_ref):
    @pl.when(pl.program_id(0) == 0)
    def _():
        carry_ref[...] = jnp.zeros_like(carry_ref)

    tm = x_ref.shape[0]
    a = _rms(fox_ref[...], gfox_ref[...]).astype(BF16)
    bb = _rms(sb_ref[...], gsb_ref[...]).astype(BF16)
    x1 = x_ref[...] + _dot(a, wo_ref[0:FOX_WIDTH, :]) + _dot(bb, wo_ref[FOX_WIDTH:, :])
    x1_ref[...] = x1
    h2 = _rms(x1, g2_ref[...])
    h2_ref[...] = h2
    h_hi, h_lo = _split2(h2)
    w_hi, w_lo = _split2(wr_ref[...])
    logits = (_dot_nt(w_hi, h_hi) + _dot_nt(w_hi, h_lo) + _dot_nt(w_lo, h_hi)
              + br_ref[...])
    e_iota = lax.broadcasted_iota(jnp.int32, (N_EXPERTS, tm), 0)
    work = logits
    vals, idxs, sels = [], [], []
    for _ in range(TOP_K):
        m = jnp.max(work, axis=0, keepdims=True)
        idx = jnp.min(jnp.where(work == m, e_iota, N_EXPERTS), axis=0, keepdims=True)
        sel = e_iota == idx
        work = jnp.where(sel, -jnp.inf, work)
        vals.append(m)
        idxs.append(idx)
        sels.append(sel)
    exps = [jnp.exp(v - vals[0]) for v in vals]
    denom = exps[0] + exps[1] + exps[2] + exps[3]
    e_ref[...] = jnp.concatenate(idxs, axis=0)
    w_ref[...] = jnp.concatenate([ex / denom for ex in exps], axis=0)
    chosen = jnp.where(sels[0] | sels[1] | sels[2] | sels[3], 1.0, 0.0)
    row = lax.broadcasted_iota(jnp.int32, (tm, tm), 0)
    col = lax.broadcasted_iota(jnp.int32, (tm, tm), 1)
    before = (row < col).astype(BF16)
    rank = _dot(chosen.astype(BF16), before) + carry_ref[...]
    rank_ref[...] = jnp.concatenate(
        [jnp.sum(jnp.where(sel, rank, 0.0), axis=0, keepdims=True) for sel in sels],
        axis=0).astype(jnp.int32)
    total = carry_ref[...] + jnp.sum(chosen, axis=1, keepdims=True)
    carry_ref[...] = total
    cnt_ref[...] = jnp.broadcast_to(total, cnt_ref.shape)


def _out_proj(x2, fox2, sb2, gfox, gsb, wo, g2, wr_t, br):
    n, d = x2.shape
    tm = min(ROW_TILE, n)
    const = lambda i: (0, 0)
    return pl.pallas_call(
        _out_proj_kernel,
        name="out_proj",
        grid=(n // tm,),
        in_specs=[
            pl.BlockSpec((tm, d), lambda i: (i, 0)),
            pl.BlockSpec((tm, FOX_WIDTH), lambda i: (i, 0)),
            pl.BlockSpec((tm, SB_WIDTH), lambda i: (i, 0)),
            pl.BlockSpec((1, FOX_WIDTH), const),
            pl.BlockSpec((1, SB_WIDTH), const),
            pl.BlockSpec((FOX_WIDTH + SB_WIDTH, d), const),
            pl.BlockSpec((1, d), const),
            pl.BlockSpec((N_EXPERTS, d), const),
            pl.BlockSpec((N_EXPERTS, 1), const),
        ],
        out_specs=[
            pl.BlockSpec((tm, d), lambda i: (i, 0)),
            pl.BlockSpec((tm, d), lambda i: (i, 0)),
            pl.BlockSpec((TOP_K, tm), lambda i: (0, i)),
            pl.BlockSpec((TOP_K, tm), lambda i: (0, i)),
            pl.BlockSpec((TOP_K, tm), lambda i: (0, i)),
            pl.BlockSpec((N_EXPERTS, LANES), const),
        ],
        out_shape=[
            jax.ShapeDtypeStruct((n, d), F32),
            jax.ShapeDtypeStruct((n, d), F32),
            jax.ShapeDtypeStruct((TOP_K, n), jnp.int32),
            jax.ShapeDtypeStruct((TOP_K, n), F32),
            jax.ShapeDtypeStruct((TOP_K, n), jnp.int32),
            jax.ShapeDtypeStruct((N_EXPERTS, LANES), F32),
        ],
        scratch_shapes=[pltpu.VMEM((N_EXPERTS, 1), F32)],
        compiler_params=pltpu.CompilerParams(
            dimension_semantics=("arbitrary",),
            vmem_limit_bytes=VMEM_LIMIT),
    )(x2, fox2, sb2, gfox, gsb, wo, g2, wr_t, br)


def _route_kernel(cnt_ref, e_ref, rank_ref, dest_ref, be_ref, nused_ref, start_ref,
                  *, n_blocks):
    def per_expert(e, pstart):
        padded = (cnt_ref[e] + EXPERT_TILE - 1) // EXPERT_TILE * EXPERT_TILE
        start_ref[e] = pstart
        first = pstart // EXPERT_TILE

        def fill(j, _):
            be_ref[first + j] = e
            return 0

        lax.fori_loop(0, padded // EXPERT_TILE, fill, 0)
        return pstart + padded

    used = lax.fori_loop(0, N_EXPERTS, per_expert, 0) // EXPERT_TILE
    nused_ref[0] = used

    def fill_tail(j, _):
        be_ref[j] = N_EXPERTS - 1
        return 0

    lax.fori_loop(used, n_blocks, fill_tail, 0)
    ev = e_ref[...]
    offs = jnp.zeros(ev.shape, jnp.int32)
    for e in range(N_EXPERTS):
        offs = jnp.where(ev == e, start_ref[e], offs)
    dest_ref[...] = rank_ref[...] + offs


def _route(counts, e_k, rank_k, n_blocks):
    smem = pl.BlockSpec(memory_space=pltpu.SMEM)
    vmem = pl.BlockSpec(memory_space=pltpu.VMEM)
    return pl.pallas_call(
        functools.partial(_route_kernel, n_blocks=n_blocks),
        name="route",
        in_specs=[smem, vmem, vmem],
        out_specs=[vmem, smem, smem],
        out_shape=[
            jax.ShapeDtypeStruct(e_k.shape, jnp.int32),
            jax.ShapeDtypeStruct((n_blocks,), jnp.int32),
            jax.ShapeDtypeStruct((1,), jnp.int32),
        ],
        scratch_shapes=[pltpu.SMEM((N_EXPERTS,), jnp.int32)],
    )(counts, e_k, rank_k)


def _dispatch_kernel(dest_ref, h_ref, xb_in_ref, xb_ref, sem, *, n_tok, tile):
    del xb_in_ref
    base = pl.program_id(0) * tile

    def copy(t, k):
        d = dest_ref[k * n_tok + base + t]
        return pltpu.make_async_copy(
            h_ref.at[pl.ds(base + t, 1), :], xb_ref.at[pl.ds(d, 1), :], sem)

    def issue(t, _):
        for k in range(TOP_K):
            copy(t, k).start()
        return 0

    def drain(t, _):
        for k in range(TOP_K):
            copy(t, k).wait()
        return 0

    lax.fori_loop(0, tile, issue, 0)
    lax.fori_loop(0, tile, drain, 0)


def _dispatch(dest_flat, h2, n_rows):
    n, d = h2.shape
    tile = min(DISPATCH_TILE, n)
    any_spec = pl.BlockSpec(memory_space=pl.ANY)
    xb0 = jnp.zeros((n_rows, d), h2.dtype)
    return pl.pallas_call(
        functools.partial(_dispatch_kernel, n_tok=n, tile=tile),
        name="dispatch",
        grid_spec=pltpu.PrefetchScalarGridSpec(
            num_scalar_prefetch=1,
            grid=(n // tile,),
            in_specs=[any_spec, any_spec],
            out_specs=any_spec,
            scratch_shapes=[pltpu.SemaphoreType.DMA(())],
        ),
        out_shape=jax.ShapeDtypeStruct((n_rows, d), h2.dtype),
        input_output_aliases={2: 0},
        compiler_params=pltpu.CompilerParams(dimension_semantics=("arbitrary",)),
    )(dest_flat, h2, xb0)


def _experts_kernel(be_ref, nused_ref, xb_ref, w1_ref, b1_ref, w2_ref, b2_ref,
                    y_ref, w1b_ref, w2b_ref):
    i = pl.program_id(0)
    d_ff = w2_ref.shape[1]

    @pl.when((i == 0) | (be_ref[i] != be_ref[jnp.maximum(i - 1, 0)]))
    def _():
        w1b_ref[...] = w1_ref[0].astype(BF16)
        w2b_ref[...] = w2_ref[0].astype(BF16)

    @pl.when(i < nused_ref[0])
    def _():
        hh = _dot(xb_ref[...].astype(BF16), w1b_ref[...]) + b1_ref[0]
        glu = jnp.minimum(hh[:, :d_ff], SWIGLU_LIMIT)
        lin = jnp.clip(hh[:, d_ff:], -SWIGLU_LIMIT, SWIGLU_LIMIT)
        act = glu * (1.0 / (1.0 + jnp.exp(-SWIGLU_ALPHA * glu))) * (lin + 1.0)
        y_ref[...] = _dot(act.astype(BF16), w2b_ref[...]) + b2_ref[0]

    @pl.when(i >= nused_ref[0])
    def _():
        y_ref[...] = jnp.zeros_like(y_ref)


def _experts(block_e, nused, xb, w1, b1, w2, b2):
    n_rows, d = xb.shape
    n_e, _, two_f = w1.shape
    d_ff = two_f // 2
    n_blocks = n_rows // EXPERT_TILE
    return pl.pallas_call(
        _experts_kernel,
        name="experts",
        grid_spec=pltpu.PrefetchScalarGridSpec(
            num_scalar_prefetch=2,
            grid=(n_blocks,),
            in_specs=[
                pl.BlockSpec((EXPERT_TILE, d), lambda i, be, nu: (i, 0)),
                pl.BlockSpec((1, d, two_f), lambda i, be, nu: (be[i], 0, 0)),
                pl.BlockSpec((1, 1, two_f), lambda i, be, nu: (be[i], 0, 0)),
                pl.BlockSpec((1, d_ff, d), lambda i, be, nu: (be[i], 0, 0)),
                pl.BlockSpec((1, 1, d), lambda i, be, nu: (be[i], 0, 0)),
            ],
            out_specs=pl.BlockSpec((EXPERT_TILE, d), lambda i, be, nu: (i, 0)),
            scratch_shapes=[pltpu.VMEM((d, two_f), BF16), pltpu.VMEM((d_ff, d), BF16)],
        ),
        out_shape=jax.ShapeDtypeStruct((n_rows, d), F32),
        compiler_params=pltpu.CompilerParams(
            dimension_semantics=("arbitrary",),
            vmem_limit_bytes=VMEM_LIMIT),
    )(block_e, nused, xb, w1, b1.reshape(n_e, 1, two_f), w2, b2.reshape(n_e, 1, d))


def _combine_kernel(dest_ref, x1_ref, w_ref, gf_ref, yb_ref, out_ref, buf_ref, sem,
                    *, n_tok):
    tile = x1_ref.shape[0]
    base = pl.program_id(0) * tile

    def copy(t, k):
        d = dest_ref[k * n_tok + base + t]
        return pltpu.make_async_copy(
            yb_ref.at[pl.ds(d, 1), :], buf_ref.at[k, pl.ds(t, 1), :], sem)

    def issue(t, _):
        for k in range(TOP_K):
            copy(t, k).start()
        return 0

    def drain(t, _):
        for k in range(TOP_K):
            copy(t, k).wait()
        return 0

    lax.fori_loop(0, tile, issue, 0)
    lax.fori_loop(0, tile, drain, 0)
    acc = x1_ref[...]
    w = w_ref[...]
    for k in range(TOP_K):
        acc = acc + buf_ref[k] * w[:, k:k + 1]
    out_ref[...] = _rms(acc, gf_ref[...])


def _combine(dest_flat, x1, w_tok, gf, yb):
    n, d = x1.shape
    tile = min(COMBINE_TILE, n)
    return pl.pallas_call(
        functools.partial(_combine_kernel, n_tok=n),
        name="combine",
        grid_spec=pltpu.PrefetchScalarGridSpec(
            num_scalar_prefetch=1,
            grid=(n // tile,),
            in_specs=[
                pl.BlockSpec((tile, d), lambda i, dr: (i, 0)),
                pl.BlockSpec((tile, TOP_K), lambda i, dr: (i, 0)),
                pl.BlockSpec((1, d), lambda i, dr: (0, 0)),
                pl.BlockSpec(memory_space=pl.ANY),
            ],
            out_specs=pl.BlockSpec((tile, d), lambda i, dr: (i, 0)),
            scratch_shapes=[pltpu.VMEM((TOP_K, tile, d), F32),
                            pltpu.SemaphoreType.DMA(())],
        ),
        out_shape=jax.ShapeDtypeStruct((n, d), F32),
        compiler_params=pltpu.CompilerParams(
            dimension_semantics=("arbitrary",),
            vmem_limit_bytes=VMEM_LIMIT),
    )(dest_flat, x1, w_tok, gf, yb)


def kernel(x, norm1_g, w_in, b_f, g_fox, g_sb, w_out, norm2_g, w_router, b_router,
           w1, b1, w2, b2, norm_f_g):
    b, s, d = x.shape
    n = b * s
    depth = norm1_g.shape[0]
    scale = HEAD_DIM ** -0.5
    fw, sw = FOX_WIDTH, SB_WIDTH
    n_blocks = -(-n * TOP_K // EXPERT_TILE) + N_EXPERTS
    n_rows = n_blocks * EXPERT_TILE
    assert depth == 1, "the combine kernel fuses the final norm into the only layer"
    wl = w_in[0]
    f0 = 3 * fw
    s0 = f0 + N_FOX_HEADS
    w_cat = jnp.concatenate(
        [wl[:, :fw] * scale, wl[:, fw:f0],
         wl[:, s0:s0 + sw] * scale, wl[:, s0 + sw:]], axis=1).astype(BF16)
    wf_t = wl[:, f0:s0].T.astype(BF16)
    proj, c = _in_proj(x, norm1_g[0][None, :], w_cat, wf_t, b_f[0][:, None])
    fox = _fox(proj, c)
    sb = _stick(proj)
    x1, h2, e_k, w_k, rank_k, counts = _out_proj(
        x.reshape(n, d), fox.reshape(n, fw), sb.reshape(n, sw),
        g_fox[0][None, :], g_sb[0][None, :], w_out[0].astype(BF16),
        norm2_g[0][None, :], w_router[0].T, b_router[0][:, None])
    dest, block_e, nused = _route(counts[:, 0].astype(jnp.int32), e_k, rank_k, n_blocks)
    dest_flat = dest.reshape(-1)
    xb = _dispatch(dest_flat, h2, n_rows)
    yb = _experts(block_e, nused, xb, w1[0], b1[0], w2[0], b2[0])
    out = _combine(dest_flat, x1, w_k.T, norm_f_g[None, :], yb)
    return out.reshape(b, s, d)
```

```python
import functools

import jax
import jax.numpy as jnp
import numpy as np
from jax import lax
from jax.experimental import pallas as pl
from jax.experimental.pallas import tpu as pltpu

HEAD_DIM = 64
N_FOX_HEADS = 8
N_SB_HEADS = 8
FOX_WIDTH = N_FOX_HEADS * HEAD_DIM
SB_WIDTH = N_SB_HEADS * HEAD_DIM
N_EXPERTS = 32
TOP_K = 4
SWIGLU_LIMIT = 7.0
SWIGLU_ALPHA = 1.702
RMS_EPS = 1e-5
LOG2E = 1.4426950408889634

LANES = 128
PAIRS = FOX_WIDTH // LANES

ROW_TILE = 256
Q_TILE = 512
K_TILE = 512
CUM_CHUNK = 256
ROW_CHUNK = 64
EXPERT_TILE = 256
DISPATCH_TILE = 512
COMBINE_TILE = 256
VMEM_LIMIT = 56 * 1024 * 1024

F32 = jnp.float32
BF16 = jnp.bfloat16


def _dot(a, b):
    return jnp.dot(a, b, preferred_element_type=F32)


def _dot_nt(a, b):
    return lax.dot_general(a, b, (((1,), (1,)), ((), ())), preferred_element_type=F32)


def _split2(v):
    hi = v.astype(BF16)
    lo = (v - hi.astype(F32)).astype(BF16)
    return hi, lo


def _split3(v):
    hi = v.astype(BF16)
    r = v - hi.astype(F32)
    mid = r.astype(BF16)
    lo = (r - mid.astype(F32)).astype(BF16)
    return hi, mid, lo


def _softplus(z):
    return jnp.maximum(z, 0.0) + jnp.log1p(jnp.exp(-jnp.abs(z)))


def _softplus2(z2):
    return jnp.maximum(z2, 0.0) + jnp.log2(1.0 + jnp.exp2(-jnp.abs(z2)))


def _rms(v, g):
    return v * lax.rsqrt(jnp.mean(v * v, axis=-1, keepdims=True) + RMS_EPS) * g


def _in_proj_kernel(x_ref, g_ref, wqk_ref, wvt_ref, wf_ref, bf_ref, sel_ref,
                    q_ref, kf_ref, ks_ref, aug_ref, vt_ref, carry_ref):
    @pl.when(pl.program_id(1) == 0)
    def _():
        carry_ref[...] = jnp.zeros_like(carry_ref)

    tm = x_ref.shape[1]
    mix = FOX_WIDTH + SB_WIDTH
    hb = _rms(x_ref[0], g_ref[...]).astype(BF16)
    qk = _dot(hb, wqk_ref[...])
    q_ref[0] = qk[:, :mix].astype(BF16)
    kf_ref[0] = qk[:, mix:mix + FOX_WIDTH].astype(BF16)
    ks_ref[0] = qk[:, mix + FOX_WIDTH:].astype(BF16)
    vt_ref[0] = _dot_nt(wvt_ref[...], hb).astype(BF16)
    log_f = -LOG2E * _softplus(-(_dot(hb, wf_ref[...]) + bf_ref[...]))
    row = lax.broadcasted_iota(jnp.int32, (tm, tm), 0)
    col = lax.broadcasted_iota(jnp.int32, (tm, tm), 1)
    tri = (col <= row).astype(BF16)
    hi, mid, lo = _split3(log_f)
    c = _dot(tri, hi) + _dot(tri, mid) + _dot(tri, lo) + carry_ref[...]
    carry_ref[...] = c[tm - 1:tm, :]
    c_hi, c_mid, c_lo = _split3(c)
    aug = _dot(c_hi, sel_ref[0]) + _dot(c_mid, sel_ref[1]) + _dot(c_lo, sel_ref[2])
    aug_ref[0] = aug.astype(BF16)


def _aug_selector():
    sel = np.zeros((3, LANES, FOX_WIDTH), np.float32)
    for t in range(3):
        for h in range(N_FOX_HEADS):
            sel[t, h, LANES * (h // 2) + 3 * (h % 2) + t] = -1.0
    return jnp.asarray(sel, BF16)


def _in_proj(x, g, wqk, wvt, wf, bf, sel):
    b, s, d = x.shape
    mix = FOX_WIDTH + SB_WIDTH
    tm = min(ROW_TILE, s)
    const2 = lambda i, j: (0, 0)
    return pl.pallas_call(
        _in_proj_kernel,
        name="in_proj",
        grid=(b, s // tm),
        in_specs=[
            pl.BlockSpec((1, tm, d), lambda i, j: (i, j, 0)),
            pl.BlockSpec((1, d), const2),
            pl.BlockSpec((d, 2 * mix), const2),
            pl.BlockSpec((mix, d), const2),
            pl.BlockSpec((d, LANES), const2),
            pl.BlockSpec((1, LANES), const2),
            pl.BlockSpec((3, LANES, FOX_WIDTH), lambda i, j: (0, 0, 0)),
        ],
        out_specs=[
            pl.BlockSpec((1, tm, mix), lambda i, j: (i, j, 0)),
            pl.BlockSpec((1, tm, FOX_WIDTH), lambda i, j: (i, j, 0)),
            pl.BlockSpec((1, tm, SB_WIDTH), lambda i, j: (i, j, 0)),
            pl.BlockSpec((1, tm, FOX_WIDTH), lambda i, j: (i, j, 0)),
            pl.BlockSpec((1, mix, tm), lambda i, j: (i, 0, j)),
        ],
        out_shape=[
            jax.ShapeDtypeStruct((b, s, mix), BF16),
            jax.ShapeDtypeStruct((b, s, FOX_WIDTH), BF16),
            jax.ShapeDtypeStruct((b, s, SB_WIDTH), BF16),
            jax.ShapeDtypeStruct((b, s, FOX_WIDTH), BF16),
            jax.ShapeDtypeStruct((b, mix, s), BF16),
        ],
        scratch_shapes=[pltpu.VMEM((1, LANES), F32)],
        compiler_params=pltpu.CompilerParams(
            dimension_semantics=("parallel", "arbitrary"),
            vmem_limit_bytes=VMEM_LIMIT),
    )(x, g, wqk, wvt, wf, bf, sel)


def _stack_heads(q, ones_cols):
    tq = q.shape[0]
    lane = lax.broadcasted_iota(jnp.int32, (tq, LANES), 1)
    low = lane < HEAD_DIM
    qf = q.astype(F32)
    halves = [jnp.where(low, qf, 0.0), jnp.where(low, 0.0, qf)]
    if ones_cols:
        halves = [jnp.concatenate([halves[hh], jnp.where(
            (lane >= 3 * hh) & (lane < 3 * hh + 3), 1.0, 0.0)], axis=1) for hh in range(2)]
    return jnp.concatenate(halves, axis=0).astype(q.dtype)


def _unstack_heads(acc_t, tq):
    sub = lax.broadcasted_iota(jnp.int32, (LANES, tq), 0)
    return jnp.where(sub < HEAD_DIM, acc_t[:, :tq], acc_t[:, tq:]).T


def _fox_kernel(q_ref, k_ref, aug_ref, vt_ref, o_ref,
                q2_ref, m_ref, l_ref, acc_ref, s_ref, p_ref, a_ref):
    i = pl.program_id(2)
    tq = q_ref.shape[1]
    q2_ref[...] = _stack_heads(q_ref[0], True)
    m_ref[...] = jnp.full(m_ref.shape, -1e30, F32)
    l_ref[...] = jnp.zeros_like(l_ref)
    acc_ref[...] = jnp.zeros_like(acc_ref)

    def scores(j, buf):
        k0 = pl.multiple_of(j * tq, tq)
        ka = jnp.concatenate(
            [k_ref[0, pl.ds(k0, tq), :], aug_ref[0, pl.ds(k0, tq), :]], axis=1)
        s_ref[buf] = _dot_nt(ka, q2_ref[...])

    def softmax_update(buf, diagonal):
        if diagonal:
            key = lax.broadcasted_iota(jnp.int32, (tq, 2 * tq), 0)
            qry = lax.broadcasted_iota(jnp.int32, (tq, 2 * tq), 1)
            qry = jnp.where(qry >= tq, qry - tq, qry)
            s_ref[buf] = jnp.where(key <= qry, s_ref[buf], -jnp.inf)
        m_old = m_ref[...]
        m_new = jnp.maximum(m_old, jnp.max(s_ref[buf], axis=0, keepdims=True))
        pr = jnp.exp2(s_ref[buf] - m_new)
        alpha = jnp.exp2(m_old - m_new)
        m_ref[...] = m_new
        l_ref[...] = l_ref[...] * alpha + jnp.sum(pr, axis=0, keepdims=True)
        p_ref[buf] = pr.astype(BF16)
        a_ref[buf] = alpha

    def accumulate(j, buf):
        k0 = pl.multiple_of(j * tq, tq)
        acc_ref[...] = (acc_ref[...] * a_ref[buf]
                        + _dot(vt_ref[0, :, pl.ds(k0, tq)], p_ref[buf]))

    def step(j, buf):
        scores(j + 1, 1 - buf)
        softmax_update(buf, False)
        accumulate(jnp.maximum(j - 1, 0), 1 - buf)

    def last(buf):
        softmax_update(buf, True)
        accumulate(jnp.maximum(i - 1, 0), 1 - buf)
        accumulate(i, buf)

    scores(0, 0)
    p_ref[1] = jnp.zeros(p_ref.shape[1:], BF16)
    a_ref[1] = jnp.ones(a_ref.shape[1:], F32)

    def two_steps(t, _):
        step(2 * t, 0)
        step(2 * t + 1, 1)
        return 0

    lax.fori_loop(0, i // 2, two_steps, 0)

    @pl.when(i % 2 == 1)
    def _():
        step(i - 1, 0)
        last(1)

    @pl.when(i % 2 == 0)
    def _():
        last(0)

    o_ref[0] = _unstack_heads(acc_ref[...] / l_ref[...], tq)


def _fox(q, kf, aug, vt):
    b, s, _ = q.shape
    tq = min(Q_TILE, s)
    return pl.pallas_call(
        _fox_kernel,
        name="fox",
        grid=(b, PAIRS, s // tq),
        in_specs=[
            pl.BlockSpec((1, tq, LANES), lambda bi, p, i: (bi, i, p)),
            pl.BlockSpec((1, s, LANES), lambda bi, p, i: (bi, 0, p)),
            pl.BlockSpec((1, s, LANES), lambda bi, p, i: (bi, 0, p)),
            pl.BlockSpec((1, LANES, s), lambda bi, p, i: (bi, p, 0)),
        ],
        out_specs=pl.BlockSpec((1, tq, LANES), lambda bi, p, i: (bi, i, p)),
        out_shape=jax.ShapeDtypeStruct((b, s, FOX_WIDTH), F32),
        scratch_shapes=[
            pltpu.VMEM((2 * tq, 2 * LANES), BF16),
            pltpu.VMEM((1, 2 * tq), F32),
            pltpu.VMEM((1, 2 * tq), F32),
            pltpu.VMEM((LANES, 2 * tq), F32),
            pltpu.VMEM((2, tq, 2 * tq), F32),
            pltpu.VMEM((2, tq, 2 * tq), BF16),
            pltpu.VMEM((2, 1, 2 * tq), F32),
        ],
        compiler_params=pltpu.CompilerParams(
            dimension_semantics=("parallel", "parallel", "parallel"),
            vmem_limit_bytes=VMEM_LIMIT),
    )(q, kf, aug, vt)


def _stick_kernel(q_ref, k_ref, vt_ref, o_ref, q2_ref, tail_ref, acc_ref, z_ref, a_ref):
    i = pl.program_id(2)
    tq = q_ref.shape[1]
    tc = min(CUM_CHUNK, tq)
    q2_ref[...] = _stack_heads(q_ref[0], False)
    tail_ref[...] = jnp.zeros_like(tail_ref)
    acc_ref[...] = jnp.zeros_like(acc_ref)
    kr = lax.broadcasted_iota(jnp.int32, (tc, tc), 0)
    kc = lax.broadcasted_iota(jnp.int32, (tc, tc), 1)
    later = (kc > kr).astype(BF16)

    def scores(j, buf):
        k0 = pl.multiple_of(j * tq, tq)
        z_ref[buf] = _dot_nt(k_ref[0, pl.ds(k0, tq), :], q2_ref[...])

    def weights(buf, diagonal):
        tail = tail_ref[...]
        for ch in reversed(range(tq // tc)):
            z_c = z_ref[buf, ch * tc:(ch + 1) * tc, :]
            sp_c = _softplus2(z_c)
            if diagonal:
                key = lax.broadcasted_iota(jnp.int32, (tc, 2 * tq), 0) + ch * tc
                qry = lax.broadcasted_iota(jnp.int32, (tc, 2 * tq), 1)
                qry = jnp.where(qry >= tq, qry - tq, qry)
                strict = key < qry
                sp_m = jnp.where(strict, sp_c, 0.0)
            else:
                sp_m = sp_c
            hi, lo = _split2(sp_m)
            suffix = _dot(later, hi) + _dot(later, lo)
            a = jnp.exp2((z_c - sp_c) - (suffix + tail))
            if diagonal:
                a = jnp.where(strict, a, 0.0)
            a_ref[buf, ch * tc:(ch + 1) * tc, :] = a.astype(BF16)
            tail = tail + suffix[0:1, :] + sp_m[0:1, :]
        tail_ref[...] = tail

    def accumulate(j, buf):
        k0 = pl.multiple_of(j * tq, tq)
        acc_ref[...] = acc_ref[...] + _dot(vt_ref[0, :, pl.ds(k0, tq)], a_ref[buf])

    def step(n, buf, diagonal):
        j = i - n
        scores(j - 1, 1 - buf)
        weights(buf, diagonal)
        if not diagonal:
            accumulate(j + 1, 1 - buf)

    def last(buf, diagonal):
        weights(buf, diagonal)
        if not diagonal:
            accumulate(1, 1 - buf)
        accumulate(0, buf)

    scores(i, 0)

    @pl.when(i == 0)
    def _():
        last(0, True)

    @pl.when(i > 0)
    def _():
        step(0, 0, True)

        def two_steps(t, _):
            step(1 + 2 * t, 1, False)
            step(2 + 2 * t, 0, False)
            return 0

        lax.fori_loop(0, (i - 1) // 2, two_steps, 0)

        @pl.when(i % 2 == 0)
        def _():
            step(i - 1, 1, False)
            last(0, False)

        @pl.when(i % 2 == 1)
        def _():
            last(1, False)

    o_ref[0] = _unstack_heads(acc_ref[...], tq)


def _stick(q, ks, vt):
    b, s, _ = q.shape
    tq = min(Q_TILE, s)
    return pl.pallas_call(
        _stick_kernel,
        name="stick",
        grid=(b, PAIRS, s // tq),
        in_specs=[
            pl.BlockSpec((1, tq, LANES), lambda bi, p, i: (bi, i, PAIRS + p)),
            pl.BlockSpec((1, s, LANES), lambda bi, p, i: (bi, 0, p)),
            pl.BlockSpec((1, LANES, s), lambda bi, p, i: (bi, PAIRS + p, 0)),
        ],
        out_specs=pl.BlockSpec((1, tq, LANES), lambda bi, p, i: (bi, i, p)),
        out_shape=jax.ShapeDtypeStruct((b, s, SB_WIDTH), F32),
        scratch_shapes=[
            pltpu.VMEM((2 * tq, LANES), BF16),
            pltpu.VMEM((1, 2 * tq), F32),
            pltpu.VMEM((LANES, 2 * tq), F32),
            pltpu.VMEM((2, tq, 2 * tq), F32),
            pltpu.VMEM((2, tq, 2 * tq), BF16),
        ],
        compiler_params=pltpu.CompilerParams(
            dimension_semantics=("parallel", "parallel", "parallel"),
            vmem_limit_bytes=VMEM_LIMIT),
    )(q, ks, vt)


def _out_proj_kernel(x_ref, fox_ref, sb_ref, gfox_ref, gsb_ref, wo_ref, g2_ref,
                     wr_ref, br_ref,
                     x1_ref, h2_ref, e_ref, w_ref, rank_ref, cnt_ref, carry_ref):
    @pl.when(pl.program_id(0) == 0)
    def _():
        carry_ref[...] = jnp.zeros_like(carry_ref)

    tm = x_ref.shape[0]
    a = _rms(fox_ref[...], gfox_ref[...]).astype(BF16)
    bb = _rms(sb_ref[...], gsb_ref[...]).astype(BF16)
    x1 = x_ref[...] + _dot(a, wo_ref[0:FOX_WIDTH, :]) + _dot(bb, wo_ref[FOX_WIDTH:, :])
    x1_ref[...] = x1
    h2 = _rms(x1, g2_ref[...])
    h2_ref[...] = h2
    h_hi, h_lo = _split2(h2)
    w_hi, w_lo = _split2(wr_ref[...])
    logits = (_dot_nt(w_hi, h_hi) + _dot_nt(w_hi, h_lo) + _dot_nt(w_lo, h_hi)
              + br_ref[...])
    e_iota = lax.broadcasted_iota(jnp.int32, (N_EXPERTS, tm), 0)
    work = logits
    vals, idxs, sels = [], [], []
    for _ in range(TOP_K):
        m = jnp.max(work, axis=0, keepdims=True)
        idx = jnp.min(jnp.where(work == m, e_iota, N_EXPERTS), axis=0, keepdims=True)
        sel = e_iota == idx
        work = jnp.where(sel, -jnp.inf, work)
        vals.append(m)
        idxs.append(idx)
        sels.append(sel)
    exps = [jnp.exp(v - vals[0]) for v in vals]
    denom = exps[0] + exps[1] + exps[2] + exps[3]
    e_ref[...] = jnp.concatenate(idxs, axis=0)
    w_ref[...] = jnp.concatenate([ex / denom for ex in exps], axis=0)
    chosen = jnp.where(sels[0] | sels[1] | sels[2] | sels[3], 1.0, 0.0)
    row = lax.broadcasted_iota(jnp.int32, (tm, tm), 0)
    col = lax.broadcasted_iota(jnp.int32, (tm, tm), 1)
    before = (row < col).astype(BF16)
    rank = _dot(chosen.astype(BF16), before) + carry_ref[...]
    rank_ref[...] = jnp.concatenate(
        [jnp.sum(jnp.where(sel, rank, 0.0), axis=0, keepdims=True) for sel in sels],
        axis=0).astype(jnp.int32)
    total = carry_ref[...] + jnp.sum(chosen, axis=1, keepdims=True)
    carry_ref[...] = total
    cnt_ref[...] = jnp.broadcast_to(total, cnt_ref.shape)


def _out_proj(x2, fox2, sb2, gfox, gsb, wo, g2, wr_t, br):
    n, d = x2.shape
    tm = min(ROW_TILE, n)
    const = lambda i: (0, 0)
    return pl.pallas_call(
        _out_proj_kernel,
        name="out_proj",
        grid=(n // tm,),
        in_specs=[
            pl.BlockSpec((tm, d), lambda i: (i, 0)),
            pl.BlockSpec((tm, FOX_WIDTH), lambda i: (i, 0)),
            pl.BlockSpec((tm, SB_WIDTH), lambda i: (i, 0)),
            pl.BlockSpec((1, FOX_WIDTH), const),
            pl.BlockSpec((1, SB_WIDTH), const),
            pl.BlockSpec((FOX_WIDTH + SB_WIDTH, d), const),
            pl.BlockSpec((1, d), const),
            pl.BlockSpec((N_EXPERTS, d), const),
            pl.BlockSpec((N_EXPERTS, 1), const),
        ],
        out_specs=[
            pl.BlockSpec((tm, d), lambda i: (i, 0)),
            pl.BlockSpec((tm, d), lambda i: (i, 0)),
            pl.BlockSpec((TOP_K, tm), lambda i: (0, i)),
            pl.BlockSpec((TOP_K, tm), lambda i: (0, i)),
            pl.BlockSpec((TOP_K, tm), lambda i: (0, i)),
            pl.BlockSpec((N_EXPERTS, LANES), const),
        ],
        out_shape=[
            jax.ShapeDtypeStruct((n, d), F32),
            jax.ShapeDtypeStruct((n, d), F32),
            jax.ShapeDtypeStruct((TOP_K, n), jnp.int32),
            jax.ShapeDtypeStruct((TOP_K, n), F32),
            jax.ShapeDtypeStruct((TOP_K, n), jnp.int32),
            jax.ShapeDtypeStruct((N_EXPERTS, LANES), F32),
        ],
        scratch_shapes=[pltpu.VMEM((N_EXPERTS, 1), F32)],
        compiler_params=pltpu.CompilerParams(
            dimension_semantics=("arbitrary",),
            vmem_limit_bytes=VMEM_LIMIT),
    )(x2, fox2, sb2, gfox, gsb, wo, g2, wr_t, br)


def _route_kernel(cnt_ref, e_ref, rank_ref, dest_ref, be_ref, nused_ref, start_ref,
                  *, n_blocks):
    def per_expert(e, pstart):
        padded = (cnt_ref[e] + EXPERT_TILE - 1) // EXPERT_TILE * EXPERT_TILE
        start_ref[e] = pstart
        first = pstart // EXPERT_TILE

        def fill(j, _):
            be_ref[first + j] = e
            return 0

        lax.fori_loop(0, padded // EXPERT_TILE, fill, 0)
        return pstart + padded

    used = lax.fori_loop(0, N_EXPERTS, per_expert, 0) // EXPERT_TILE
    nused_ref[0] = used

    def fill_tail(j, _):
        be_ref[j] = N_EXPERTS - 1
        return 0

    lax.fori_loop(used, n_blocks, fill_tail, 0)
    ev = e_ref[...]
    offs = jnp.zeros(ev.shape, jnp.int32)
    for e in range(N_EXPERTS):
        offs = jnp.where(ev == e, start_ref[e], offs)
    dest_ref[...] = rank_ref[...] + offs


def _route(counts, e_k, rank_k, n_blocks):
    smem = pl.BlockSpec(memory_space=pltpu.SMEM)
    vmem = pl.BlockSpec(memory_space=pltpu.VMEM)
    return pl.pallas_call(
        functools.partial(_route_kernel, n_blocks=n_blocks),
        name="route",
        in_specs=[smem, vmem, vmem],
        out_specs=[vmem, smem, smem],
        out_shape=[
            jax.ShapeDtypeStruct(e_k.shape, jnp.int32),
            jax.ShapeDtypeStruct((n_blocks,), jnp.int32),
            jax.ShapeDtypeStruct((1,), jnp.int32),
        ],
        scratch_shapes=[pltpu.SMEM((N_EXPERTS,), jnp.int32)],
    )(counts, e_k, rank_k)


def _dispatch_kernel(dest_ref, h_ref, xb_in_ref, xb_ref, sem, *, n_tok, tile):
    del xb_in_ref
    base = pl.program_id(0) * tile

    def copy(t, k):
        d = dest_ref[k * n_tok + base + t]
        return pltpu.make_async_copy(
            h_ref.at[pl.ds(t, 1), :], xb_ref.at[pl.ds(d, 1), :], sem)

    def issue(t, _):
        for k in range(TOP_K):
            copy(t, k).start()
        return 0

    def drain(t, _):
        for k in range(TOP_K):
            copy(t, k).wait()
        return 0

    lax.fori_loop(0, tile, issue, 0, unroll=8)
    lax.fori_loop(0, tile, drain, 0, unroll=8)


def _dispatch(dest_flat, h2, n_rows):
    n, d = h2.shape
    tile = min(DISPATCH_TILE, n)
    any_spec = pl.BlockSpec(memory_space=pl.ANY)
    xb0 = jnp.zeros((n_rows, d), h2.dtype)
    return pl.pallas_call(
        functools.partial(_dispatch_kernel, n_tok=n, tile=tile),
        name="dispatch",
        grid_spec=pltpu.PrefetchScalarGridSpec(
            num_scalar_prefetch=1,
            grid=(n // tile,),
            in_specs=[pl.BlockSpec((tile, d), lambda i, dr: (i, 0)), any_spec],
            out_specs=any_spec,
            scratch_shapes=[pltpu.SemaphoreType.DMA(())],
        ),
        out_shape=jax.ShapeDtypeStruct((n_rows, d), h2.dtype),
        input_output_aliases={2: 0},
        compiler_params=pltpu.CompilerParams(dimension_semantics=("arbitrary",)),
    )(dest_flat, h2, xb0)


def _experts_kernel(be_ref, nused_ref, xb_ref, w1_ref, b1_ref, w2_ref, b2_ref,
                    y_ref, w1b_ref, w2b_ref):
    i = pl.program_id(0)
    d_ff = w2_ref.shape[1]

    @pl.when((i == 0) | (be_ref[i] != be_ref[jnp.maximum(i - 1, 0)]))
    def _():
        w1b_ref[...] = w1_ref[0].astype(BF16)
        w2b_ref[...] = w2_ref[0].astype(BF16)

    @pl.when(i < nused_ref[0])
    def _():
        hh = _dot(xb_ref[...].astype(BF16), w1b_ref[...]) + b1_ref[0]
        glu = jnp.minimum(hh[:, :d_ff], SWIGLU_LIMIT)
        lin = jnp.clip(hh[:, d_ff:], -SWIGLU_LIMIT, SWIGLU_LIMIT)
        act = glu * (1.0 / (1.0 + jnp.exp(-SWIGLU_ALPHA * glu))) * (lin + 1.0)
        y_ref[...] = _dot(act.astype(BF16), w2b_ref[...]) + b2_ref[0]

    @pl.when(i >= nused_ref[0])
    def _():
        y_ref[...] = jnp.zeros_like(y_ref)


def _experts(block_e, nused, xb, w1, b1, w2, b2):
    n_rows, d = xb.shape
    n_e, _, two_f = w1.shape
    d_ff = two_f // 2
    n_blocks = n_rows // EXPERT_TILE
    return pl.pallas_call(
        _experts_kernel,
        name="experts",
        grid_spec=pltpu.PrefetchScalarGridSpec(
            num_scalar_prefetch=2,
            grid=(n_blocks,),
            in_specs=[
                pl.BlockSpec((EXPERT_TILE, d), lambda i, be, nu: (i, 0)),
                pl.BlockSpec((1, d, two_f), lambda i, be, nu: (be[i], 0, 0)),
                pl.BlockSpec((1, 1, two_f), lambda i, be, nu: (be[i], 0, 0)),
                pl.BlockSpec((1, d_ff, d), lambda i, be, nu: (be[i], 0, 0)),
                pl.BlockSpec((1, 1, d), lambda i, be, nu: (be[i], 0, 0)),
            ],
            out_specs=pl.BlockSpec((EXPERT_TILE, d), lambda i, be, nu: (i, 0)),
            scratch_shapes=[pltpu.VMEM((d, two_f), BF16), pltpu.VMEM((d_ff, d), BF16)],
        ),
        out_shape=jax.ShapeDtypeStruct((n_rows, d), F32),
        compiler_params=pltpu.CompilerParams(
            dimension_semantics=("arbitrary",),
            vmem_limit_bytes=VMEM_LIMIT),
    )(block_e, nused, xb, w1, b1.reshape(n_e, 1, two_f), w2, b2.reshape(n_e, 1, d))


def _combine_kernel(dest_ref, x1_ref, w_ref, gf_ref, yb_ref, out_ref, buf_ref, sem,
                    *, n_tok):
    tile = x1_ref.shape[0]
    base = pl.program_id(0) * tile

    def copy(t, k):
        d = dest_ref[k * n_tok + base + t]
        return pltpu.make_async_copy(
            yb_ref.at[pl.ds(d, 1), :], buf_ref.at[k, pl.ds(t, 1), :], sem)

    def issue(t, _):
        for k in range(TOP_K):
            copy(t, k).start()
        return 0

    def drain(t, _):
        for k in range(TOP_K):
            copy(t, k).wait()
        return 0

    lax.fori_loop(0, tile, issue, 0, unroll=8)
    lax.fori_loop(0, tile, drain, 0, unroll=8)
    acc = x1_ref[...]
    w = w_ref[...]
    for k in range(TOP_K):
        acc = acc + buf_ref[k] * w[:, k:k + 1]
    out_ref[...] = _rms(acc, gf_ref[...])


def _combine(dest_flat, x1, w_tok, gf, yb):
    n, d = x1.shape
    tile = min(COMBINE_TILE, n)
    return pl.pallas_call(
        functools.partial(_combine_kernel, n_tok=n),
        name="combine",
        grid_spec=pltpu.PrefetchScalarGridSpec(
            num_scalar_prefetch=1,
            grid=(n // tile,),
            in_specs=[
                pl.BlockSpec((tile, d), lambda i, dr: (i, 0)),
                pl.BlockSpec((tile, TOP_K), lambda i, dr: (i, 0)),
                pl.BlockSpec((1, d), lambda i, dr: (0, 0)),
                pl.BlockSpec(memory_space=pl.ANY),
            ],
            out_specs=pl.BlockSpec((tile, d), lambda i, dr: (i, 0)),
            scratch_shapes=[pltpu.VMEM((TOP_K, tile, d), F32),
                            pltpu.SemaphoreType.DMA(())],
        ),
        out_shape=jax.ShapeDtypeStruct((n, d), F32),
        compiler_params=pltpu.CompilerParams(
            dimension_semantics=("arbitrary",),
            vmem_limit_bytes=VMEM_LIMIT),
    )(dest_flat, x1, w_tok, gf, yb)


def kernel(x, norm1_g, w_in, b_f, g_fox, g_sb, w_out, norm2_g, w_router, b_router,
           w1, b1, w2, b2, norm_f_g):
    b, s, d = x.shape
    n = b * s
    depth = norm1_g.shape[0]
    scale = HEAD_DIM ** -0.5 * LOG2E
    fw, sw = FOX_WIDTH, SB_WIDTH
    n_blocks = -(-n * TOP_K // EXPERT_TILE) + N_EXPERTS
    n_rows = n_blocks * EXPERT_TILE
    assert depth == 1, "the combine kernel fuses the final norm into the only layer"
    wl = w_in[0]
    f0 = 3 * fw
    s0 = f0 + N_FOX_HEADS
    w_qk = jnp.concatenate(
        [wl[:, :fw] * scale, wl[:, s0:s0 + sw] * scale,
         wl[:, fw:2 * fw], wl[:, s0 + sw:s0 + 2 * sw]], axis=1).astype(BF16)
    w_vt = jnp.concatenate([wl[:, 2 * fw:f0], wl[:, s0 + 2 * sw:]], axis=1).T.astype(BF16)
    w_f = jnp.pad(wl[:, f0:s0], ((0, 0), (0, LANES - N_FOX_HEADS))).astype(BF16)
    bias_f = jnp.pad(b_f[0], (0, LANES - N_FOX_HEADS))[None, :]
    q, kf, ks, aug, vt = _in_proj(x, norm1_g[0][None, :], w_qk, w_vt, w_f, bias_f,
                                  _aug_selector())
    fox = _fox(q, kf, aug, vt)
    sb = _stick(q, ks, vt)
    x1, h2, e_k, w_k, rank_k, counts = _out_proj(
        x.reshape(n, d), fox.reshape(n, fw), sb.reshape(n, sw),
        g_fox[0][None, :], g_sb[0][None, :], w_out[0].astype(BF16),
        norm2_g[0][None, :], w_router[0].T, b_router[0][:, None])
    dest, block_e, nused = _route(counts[:, 0].astype(jnp.int32), e_k, rank_k, n_blocks)
    dest_flat = dest.reshape(-1)
    xb = _dispatch(dest_flat, h2, n_rows)
    yb = _experts(block_e, nused, xb, w1[0], b1[0], w2[0], b2[0])
    out = _combine(dest_flat, x1, w_k.T, norm_f_g[None, :], yb)
    return out.reshape(b, s, d)
```

```python
import functools

import jax
import jax.numpy as jnp
import numpy as np
from jax import lax
from jax.experimental import pallas as pl
from jax.experimental.pallas import tpu as pltpu

HEAD_DIM = 64
N_FOX_HEADS = 8
N_SB_HEADS = 8
FOX_WIDTH = N_FOX_HEADS * HEAD_DIM
SB_WIDTH = N_SB_HEADS * HEAD_DIM
N_EXPERTS = 32
TOP_K = 4
SWIGLU_LIMIT = 7.0
SWIGLU_ALPHA = 1.702
RMS_EPS = 1e-5
LOG2E = 1.4426950408889634

LANES = 128
SUBLANES = 8
PAIRS = FOX_WIDTH // LANES

ROW_TILE = 256
Q_TILE = 512
K_TILE = 512
CUM_CHUNK = 256
ROW_CHUNK = 64
EXPERT_TILE = 256
DISPATCH_TILE = 512
COMBINE_TILE = 256
VMEM_LIMIT = 56 * 1024 * 1024

F32 = jnp.float32
BF16 = jnp.bfloat16


def _dot(a, b):
    return jnp.dot(a, b, preferred_element_type=F32)


def _dot_nt(a, b):
    return lax.dot_general(a, b, (((1,), (1,)), ((), ())), preferred_element_type=F32)


def _split2(v):
    hi = v.astype(BF16)
    lo = (v - hi.astype(F32)).astype(BF16)
    return hi, lo


def _split3(v):
    hi = v.astype(BF16)
    r = v - hi.astype(F32)
    mid = r.astype(BF16)
    lo = (r - mid.astype(F32)).astype(BF16)
    return hi, mid, lo


def _softplus(z):
    return jnp.maximum(z, 0.0) + jnp.log1p(jnp.exp(-jnp.abs(z)))


def _softplus2(z2):
    sign_bit = jnp.uint32(0x80000000)
    neg_abs = lax.bitcast_convert_type(lax.bitcast_convert_type(z2, jnp.uint32) | sign_bit, F32)
    return jnp.maximum(z2, 0.0) + jnp.log2(1.0 + jnp.exp2(neg_abs))


def _load_row_tiles(ref, rows, first=0):
    return jnp.concatenate(
        [ref[pl.ds(first + c, rows, stride=SUBLANES), :] for c in range(SUBLANES)], axis=1)


def _store_row_tiles(ref, value):
    rows = value.shape[0]
    for c in range(SUBLANES):
        ref[pl.ds(c, rows, stride=SUBLANES), :] = value[:, c * LANES:(c + 1) * LANES]


def _rms(v, g):
    return v * lax.rsqrt(jnp.mean(v * v, axis=-1, keepdims=True) + RMS_EPS) * g


def _in_proj_kernel(x_ref, g_ref, wqk_ref, wvt_ref, wf_ref, bf_ref, sel_ref,
                    q_ref, kf_ref, ks_ref, aug_ref, vt_ref, carry_ref):
    @pl.when(pl.program_id(1) == 0)
    def _():
        carry_ref[...] = jnp.zeros_like(carry_ref)

    tm = x_ref.shape[1]
    mix = FOX_WIDTH + SB_WIDTH
    hb = _rms(x_ref[0], g_ref[...]).astype(BF16)
    qk = _dot(hb, wqk_ref[...])
    q_ref[0] = qk[:, :mix].astype(BF16)
    kf_ref[0] = qk[:, mix:mix + FOX_WIDTH].astype(BF16)
    ks_ref[0] = qk[:, mix + FOX_WIDTH:].astype(BF16)
    vt_ref[0] = _dot_nt(wvt_ref[...], hb).astype(BF16)
    log_f = -LOG2E * _softplus(-(_dot(hb, wf_ref[...]) + bf_ref[...]))
    row = lax.broadcasted_iota(jnp.int32, (tm, tm), 0)
    col = lax.broadcasted_iota(jnp.int32, (tm, tm), 1)
    tri = (col <= row).astype(BF16)
    hi, mid, lo = _split3(log_f)
    c = _dot(tri, hi) + _dot(tri, mid) + _dot(tri, lo) + carry_ref[...]
    carry_ref[...] = c[tm - 1:tm, :]
    c_hi, c_mid, c_lo = _split3(c)
    aug = _dot(c_hi, sel_ref[0]) + _dot(c_mid, sel_ref[1]) + _dot(c_lo, sel_ref[2])
    aug_ref[0] = aug.astype(BF16)


def _aug_selector():
    sel = np.zeros((3, LANES, FOX_WIDTH), np.float32)
    for t in range(3):
        for h in range(N_FOX_HEADS):
            sel[t, h, LANES * (h // 2) + 3 * (h % 2) + t] = -1.0
    return jnp.asarray(sel, BF16)


def _in_proj(x, g, wqk, wvt, wf, bf, sel):
    b, s, d = x.shape
    mix = FOX_WIDTH + SB_WIDTH
    tm = min(ROW_TILE, s)
    const2 = lambda i, j: (0, 0)
    return pl.pallas_call(
        _in_proj_kernel,
        name="in_proj",
        grid=(b, s // tm),
        in_specs=[
            pl.BlockSpec((1, tm, d), lambda i, j: (i, j, 0)),
            pl.BlockSpec((1, d), const2),
            pl.BlockSpec((d, 2 * mix), const2),
            pl.BlockSpec((mix, d), const2),
            pl.BlockSpec((d, LANES), const2),
            pl.BlockSpec((1, LANES), const2),
            pl.BlockSpec((3, LANES, FOX_WIDTH), lambda i, j: (0, 0, 0)),
        ],
        out_specs=[
            pl.BlockSpec((1, tm, mix), lambda i, j: (i, j, 0)),
            pl.BlockSpec((1, tm, FOX_WIDTH), lambda i, j: (i, j, 0)),
            pl.BlockSpec((1, tm, SB_WIDTH), lambda i, j: (i, j, 0)),
            pl.BlockSpec((1, tm, FOX_WIDTH), lambda i, j: (i, j, 0)),
            pl.BlockSpec((1, mix, tm), lambda i, j: (i, 0, j)),
        ],
        out_shape=[
            jax.ShapeDtypeStruct((b, s, mix), BF16),
            jax.ShapeDtypeStruct((b, s, FOX_WIDTH), BF16),
            jax.ShapeDtypeStruct((b, s, SB_WIDTH), BF16),
            jax.ShapeDtypeStruct((b, s, FOX_WIDTH), BF16),
            jax.ShapeDtypeStruct((b, mix, s), BF16),
        ],
        scratch_shapes=[pltpu.VMEM((1, LANES), F32)],
        compiler_params=pltpu.CompilerParams(
            dimension_semantics=("parallel", "arbitrary"),
            vmem_limit_bytes=VMEM_LIMIT),
    )(x, g, wqk, wvt, wf, bf, sel)


def _stack_heads(q, ones_cols):
    tq = q.shape[0]
    lane = lax.broadcasted_iota(jnp.int32, (tq, LANES), 1)
    low = lane < HEAD_DIM
    qf = q.astype(F32)
    halves = [jnp.where(low, qf, 0.0), jnp.where(low, 0.0, qf)]
    if ones_cols:
        halves = [jnp.concatenate([halves[hh], jnp.where(
            (lane >= 3 * hh) & (lane < 3 * hh + 3), 1.0, 0.0)], axis=1) for hh in range(2)]
    return jnp.concatenate(halves, axis=0).astype(q.dtype)


def _unstack_heads(acc_t, tq):
    sub = lax.broadcasted_iota(jnp.int32, (LANES, tq), 0)
    return jnp.where(sub < HEAD_DIM, acc_t[:, :tq], acc_t[:, tq:]).T


def _fox_kernel(q_ref, k_ref, aug_ref, vt_ref, o_ref,
                q2_ref, m_ref, l_ref, acc_ref, s_ref, p_ref, a_ref):
    i = pl.program_id(2)
    tq = q_ref.shape[1]
    q2_ref[...] = _stack_heads(q_ref[0], True)
    m_ref[...] = jnp.full(m_ref.shape, -1e30, F32)
    l_ref[...] = jnp.zeros_like(l_ref)
    acc_ref[...] = jnp.zeros_like(acc_ref)

    def scores(j, buf):
        k0 = pl.multiple_of(j * tq, tq)
        ka = jnp.concatenate(
            [k_ref[0, pl.ds(k0, tq), :], aug_ref[0, pl.ds(k0, tq), :]], axis=1)
        s_ref[buf] = _dot_nt(ka, q2_ref[...])

    def softmax_update(buf, diagonal):
        if diagonal:
            key = lax.broadcasted_iota(jnp.int32, (tq, 2 * tq), 0)
            qry = lax.broadcasted_iota(jnp.int32, (tq, 2 * tq), 1)
            qry = jnp.where(qry >= tq, qry - tq, qry)
            s_ref[buf] = jnp.where(key <= qry, s_ref[buf], -jnp.inf)
        m_old = m_ref[...]
        m_new = jnp.maximum(m_old, jnp.max(s_ref[buf], axis=0, keepdims=True))
        pr = jnp.exp2(s_ref[buf] - m_new)
        alpha = jnp.exp2(m_old - m_new)
        m_ref[...] = m_new
        l_ref[...] = l_ref[...] * alpha + jnp.sum(pr, axis=0, keepdims=True)
        p_ref[buf] = pr.astype(BF16)
        a_ref[buf] = alpha

    def accumulate(j, buf):
        k0 = pl.multiple_of(j * tq, tq)
        acc_ref[...] = (acc_ref[...] * a_ref[buf]
                        + _dot(vt_ref[0, :, pl.ds(k0, tq)], p_ref[buf]))

    def step(j, buf):
        scores(j + 1, 1 - buf)
        softmax_update(buf, False)
        accumulate(jnp.maximum(j - 1, 0), 1 - buf)

    def last(buf):
        softmax_update(buf, True)
        accumulate(jnp.maximum(i - 1, 0), 1 - buf)
        accumulate(i, buf)

    scores(0, 0)
    p_ref[1] = jnp.zeros(p_ref.shape[1:], BF16)
    a_ref[1] = jnp.ones(a_ref.shape[1:], F32)

    def two_steps(t, _):
        step(2 * t, 0)
        step(2 * t + 1, 1)
        return 0

    lax.fori_loop(0, i // 2, two_steps, 0)

    @pl.when(i % 2 == 1)
    def _():
        step(i - 1, 0)
        last(1)

    @pl.when(i % 2 == 0)
    def _():
        last(0)

    o_ref[0] = _unstack_heads(acc_ref[...] / l_ref[...], tq)


def _fox(q, kf, aug, vt):
    b, s, _ = q.shape
    tq = min(Q_TILE, s)
    return pl.pallas_call(
        _fox_kernel,
        name="fox",
        grid=(b, PAIRS, s // tq),
        in_specs=[
            pl.BlockSpec((1, tq, LANES), lambda bi, p, i: (bi, i, p)),
            pl.BlockSpec((1, s, LANES), lambda bi, p, i: (bi, 0, p)),
            pl.BlockSpec((1, s, LANES), lambda bi, p, i: (bi, 0, p)),
            pl.BlockSpec((1, LANES, s), lambda bi, p, i: (bi, p, 0)),
        ],
        out_specs=pl.BlockSpec((1, tq, LANES), lambda bi, p, i: (bi, i, p)),
        out_shape=jax.ShapeDtypeStruct((b, s, FOX_WIDTH), F32),
        scratch_shapes=[
            pltpu.VMEM((2 * tq, 2 * LANES), BF16),
            pltpu.VMEM((1, 2 * tq), F32),
            pltpu.VMEM((1, 2 * tq), F32),
            pltpu.VMEM((LANES, 2 * tq), F32),
            pltpu.VMEM((2, tq, 2 * tq), F32),
            pltpu.VMEM((2, tq, 2 * tq), BF16),
            pltpu.VMEM((2, 1, 2 * tq), F32),
        ],
        compiler_params=pltpu.CompilerParams(
            dimension_semantics=("parallel", "parallel", "parallel"),
            vmem_limit_bytes=VMEM_LIMIT),
    )(q, kf, aug, vt)


def _stick_kernel(q_ref, k_ref, vt_ref, o_ref, q2_ref, tail_ref, acc_ref, z_ref, a_ref):
    i = pl.program_id(2)
    tq = q_ref.shape[1]
    tc = min(CUM_CHUNK, tq)
    q2_ref[...] = _stack_heads(q_ref[0], False)
    tail_ref[...] = jnp.zeros_like(tail_ref)
    acc_ref[...] = jnp.zeros_like(acc_ref)
    kr = lax.broadcasted_iota(jnp.int32, (tc, tc), 0)
    kc = lax.broadcasted_iota(jnp.int32, (tc, tc), 1)
    later = (kc > kr).astype(BF16)

    def scores(j, buf):
        k0 = pl.multiple_of(j * tq, tq)
        z_ref[buf] = _dot_nt(k_ref[0, pl.ds(k0, tq), :], q2_ref[...])

    def weights(buf, diagonal):
        tail = tail_ref[...]
        for ch in reversed(range(tq // tc)):
            z_c = z_ref[buf, ch * tc:(ch + 1) * tc, :]
            sp_c = _softplus2(z_c)
            if diagonal:
                key = lax.broadcasted_iota(jnp.int32, (tc, 2 * tq), 0) + ch * tc
                qry = lax.broadcasted_iota(jnp.int32, (tc, 2 * tq), 1)
                qry = jnp.where(qry >= tq, qry - tq, qry)
                strict = key < qry
                sp_m = jnp.where(strict, sp_c, 0.0)
            else:
                sp_m = sp_c
            suffix = _dot(later, sp_m.astype(BF16))
            a = jnp.exp2((z_c - sp_c) - (suffix + tail))
            if diagonal:
                a = jnp.where(strict, a, 0.0)
            a_ref[buf, ch * tc:(ch + 1) * tc, :] = a.astype(BF16)
            tail = tail + suffix[0:1, :] + sp_m[0:1, :]
        tail_ref[...] = tail

    def accumulate(j, buf):
        k0 = pl.multiple_of(j * tq, tq)
        acc_ref[...] = acc_ref[...] + _dot(vt_ref[0, :, pl.ds(k0, tq)], a_ref[buf])

    def step(n, buf, diagonal):
        j = i - n
        scores(j - 1, 1 - buf)
        weights(buf, diagonal)
        if not diagonal:
            accumulate(j + 1, 1 - buf)

    def last(buf, diagonal):
        weights(buf, diagonal)
        if not diagonal:
            accumulate(1, 1 - buf)
        accumulate(0, buf)

    scores(i, 0)

    @pl.when(i == 0)
    def _():
        last(0, True)

    @pl.when(i > 0)
    def _():
        step(0, 0, True)

        def two_steps(t, _):
            step(1 + 2 * t, 1, False)
            step(2 + 2 * t, 0, False)
            return 0

        lax.fori_loop(0, (i - 1) // 2, two_steps, 0)

        @pl.when(i % 2 == 0)
        def _():
            step(i - 1, 1, False)
            last(0, False)

        @pl.when(i % 2 == 1)
        def _():
            last(1, False)

    o_ref[0] = _unstack_heads(acc_ref[...], tq)


def _stick(q, ks, vt):
    b, s, _ = q.shape
    tq = min(Q_TILE, s)
    return pl.pallas_call(
        _stick_kernel,
        name="stick",
        grid=(b, PAIRS, s // tq),
        in_specs=[
            pl.BlockSpec((1, tq, LANES), lambda bi, p, i: (bi, i, PAIRS + p)),
            pl.BlockSpec((1, s, LANES), lambda bi, p, i: (bi, 0, p)),
            pl.BlockSpec((1, LANES, s), lambda bi, p, i: (bi, PAIRS + p, 0)),
        ],
        out_specs=pl.BlockSpec((1, tq, LANES), lambda bi, p, i: (bi, i, p)),
        out_shape=jax.ShapeDtypeStruct((b, s, SB_WIDTH), F32),
        scratch_shapes=[
            pltpu.VMEM((2 * tq, LANES), BF16),
            pltpu.VMEM((1, 2 * tq), F32),
            pltpu.VMEM((LANES, 2 * tq), F32),
            pltpu.VMEM((2, tq, 2 * tq), F32),
            pltpu.VMEM((2, tq, 2 * tq), BF16),
        ],
        compiler_params=pltpu.CompilerParams(
            dimension_semantics=("parallel", "parallel", "parallel"),
            vmem_limit_bytes=VMEM_LIMIT),
    )(q, ks, vt)


def _out_proj_kernel(x_ref, fox_ref, sb_ref, gfox_ref, gsb_ref, wo_ref, g2_ref,
                     wr_ref, br_ref,
                     x1_ref, h2_ref, e_ref, w_ref, rank_ref, cnt_ref, carry_ref):
    @pl.when(pl.program_id(0) == 0)
    def _():
        carry_ref[...] = jnp.zeros_like(carry_ref)

    tm = x_ref.shape[0]
    a = _rms(fox_ref[...], gfox_ref[...]).astype(BF16)
    bb = _rms(sb_ref[...], gsb_ref[...]).astype(BF16)
    x1 = x_ref[...] + _dot(a, wo_ref[0:FOX_WIDTH, :]) + _dot(bb, wo_ref[FOX_WIDTH:, :])
    x1_ref[...] = x1
    h2 = _rms(x1, g2_ref[...])
    _store_row_tiles(h2_ref, h2)
    h_hi, h_lo = _split2(h2)
    w_hi, w_lo = _split2(wr_ref[...])
    logits = (_dot_nt(w_hi, h_hi) + _dot_nt(w_hi, h_lo) + _dot_nt(w_lo, h_hi)
              + br_ref[...])
    e_iota = lax.broadcasted_iota(jnp.int32, (N_EXPERTS, tm), 0)
    work = logits
    vals, idxs, sels = [], [], []
    for _ in range(TOP_K):
        m = jnp.max(work, axis=0, keepdims=True)
        idx = jnp.min(jnp.where(work == m, e_iota, N_EXPERTS), axis=0, keepdims=True)
        sel = e_iota == idx
        work = jnp.where(sel, -jnp.inf, work)
        vals.append(m)
        idxs.append(idx)
        sels.append(sel)
    exps = [jnp.exp(v - vals[0]) for v in vals]
    denom = exps[0] + exps[1] + exps[2] + exps[3]
    e_ref[...] = jnp.concatenate(idxs, axis=0)
    w_ref[...] = jnp.concatenate([ex / denom for ex in exps], axis=0)
    chosen = jnp.where(sels[0] | sels[1] | sels[2] | sels[3], 1.0, 0.0)
    row = lax.broadcasted_iota(jnp.int32, (tm, tm), 0)
    col = lax.broadcasted_iota(jnp.int32, (tm, tm), 1)
    before = (row < col).astype(BF16)
    rank = _dot(chosen.astype(BF16), before) + carry_ref[...]
    rank_ref[...] = jnp.concatenate(
        [jnp.sum(jnp.where(sel, rank, 0.0), axis=0, keepdims=True) for sel in sels],
        axis=0).astype(jnp.int32)
    total = carry_ref[...] + jnp.sum(chosen, axis=1, keepdims=True)
    carry_ref[...] = total
    cnt_ref[...] = jnp.broadcast_to(total, cnt_ref.shape)


def _out_proj(x2, fox2, sb2, gfox, gsb, wo, g2, wr_t, br):
    n, d = x2.shape
    tm = min(ROW_TILE, n)
    const = lambda i: (0, 0)
    return pl.pallas_call(
        _out_proj_kernel,
        name="out_proj",
        grid=(n // tm,),
        in_specs=[
            pl.BlockSpec((tm, d), lambda i: (i, 0)),
            pl.BlockSpec((tm, FOX_WIDTH), lambda i: (i, 0)),
            pl.BlockSpec((tm, SB_WIDTH), lambda i: (i, 0)),
            pl.BlockSpec((1, FOX_WIDTH), const),
            pl.BlockSpec((1, SB_WIDTH), const),
            pl.BlockSpec((FOX_WIDTH + SB_WIDTH, d), const),
            pl.BlockSpec((1, d), const),
            pl.BlockSpec((N_EXPERTS, d), const),
            pl.BlockSpec((N_EXPERTS, 1), const),
        ],
        out_specs=[
            pl.BlockSpec((tm, d), lambda i: (i, 0)),
            pl.BlockSpec((tm * SUBLANES, LANES), lambda i: (i, 0)),
            pl.BlockSpec((TOP_K, tm), lambda i: (0, i)),
            pl.BlockSpec((TOP_K, tm), lambda i: (0, i)),
            pl.BlockSpec((TOP_K, tm), lambda i: (0, i)),
            pl.BlockSpec((N_EXPERTS, LANES), const),
        ],
        out_shape=[
            jax.ShapeDtypeStruct((n, d), F32),
            jax.ShapeDtypeStruct((n * SUBLANES, LANES), F32),
            jax.ShapeDtypeStruct((TOP_K, n), jnp.int32),
            jax.ShapeDtypeStruct((TOP_K, n), F32),
            jax.ShapeDtypeStruct((TOP_K, n), jnp.int32),
            jax.ShapeDtypeStruct((N_EXPERTS, LANES), F32),
        ],
        scratch_shapes=[pltpu.VMEM((N_EXPERTS, 1), F32)],
        compiler_params=pltpu.CompilerParams(
            dimension_semantics=("arbitrary",),
            vmem_limit_bytes=VMEM_LIMIT),
    )(x2, fox2, sb2, gfox, gsb, wo, g2, wr_t, br)


def _route_kernel(cnt_ref, e_ref, rank_ref, dest_ref, be_ref, nused_ref, start_ref,
                  *, n_blocks):
    def per_expert(e, pstart):
        padded = (cnt_ref[e] + EXPERT_TILE - 1) // EXPERT_TILE * EXPERT_TILE
        start_ref[e] = pstart
        first = pstart // EXPERT_TILE

        def fill(j, _):
            be_ref[first + j] = e
            return 0

        lax.fori_loop(0, padded // EXPERT_TILE, fill, 0)
        return pstart + padded

    used = lax.fori_loop(0, N_EXPERTS, per_expert, 0) // EXPERT_TILE
    nused_ref[0] = used

    def fill_tail(j, _):
        be_ref[j] = N_EXPERTS - 1
        return 0

    lax.fori_loop(used, n_blocks, fill_tail, 0)
    ev = e_ref[...]
    offs = jnp.zeros(ev.shape, jnp.int32)
    for e in range(N_EXPERTS):
        offs = jnp.where(ev == e, start_ref[e], offs)
    dest_ref[...] = rank_ref[...] + offs


def _route(counts, e_k, rank_k, n_blocks):
    smem = pl.BlockSpec(memory_space=pltpu.SMEM)
    vmem = pl.BlockSpec(memory_space=pltpu.VMEM)
    return pl.pallas_call(
        functools.partial(_route_kernel, n_blocks=n_blocks),
        name="route",
        in_specs=[smem, vmem, vmem],
        out_specs=[vmem, smem, smem],
        out_shape=[
            jax.ShapeDtypeStruct(e_k.shape, jnp.int32),
            jax.ShapeDtypeStruct((n_blocks,), jnp.int32),
            jax.ShapeDtypeStruct((1,), jnp.int32),
        ],
        scratch_shapes=[pltpu.SMEM((N_EXPERTS,), jnp.int32)],
    )(counts, e_k, rank_k)


def _dispatch_kernel(dest_ref, h_ref, xb_in_ref, xb_ref, sem, *, n_tok, tile):
    del xb_in_ref
    base = pl.program_id(0) * tile

    def issue(t, _):
        src = h_ref.at[pl.ds(pl.multiple_of(t * SUBLANES, SUBLANES), SUBLANES), :]
        for k in range(TOP_K):
            d = dest_ref[k * n_tok + base + t]
            pltpu.make_async_copy(
                src, xb_ref.at[pl.ds(pl.multiple_of(d * SUBLANES, SUBLANES), SUBLANES), :],
                sem).start(priority=k % 2)
        return 0

    lax.fori_loop(0, tile, issue, 0, unroll=8)
    rows = xb_ref.at[pl.ds(0, TOP_K * tile * SUBLANES), :]
    pltpu.make_async_copy(rows, rows, sem).wait()


def _dispatch(dest_flat, h2, n_rows):
    n = h2.shape[0] // SUBLANES
    tile = min(DISPATCH_TILE, n)
    any_spec = pl.BlockSpec(memory_space=pl.ANY)
    xb0 = jnp.zeros((n_rows * SUBLANES, LANES), h2.dtype)
    return pl.pallas_call(
        functools.partial(_dispatch_kernel, n_tok=n, tile=tile),
        name="dispatch",
        grid_spec=pltpu.PrefetchScalarGridSpec(
            num_scalar_prefetch=1,
            grid=(n // tile,),
            in_specs=[pl.BlockSpec((tile * SUBLANES, LANES), lambda i, dr: (i, 0)),
                      any_spec],
            out_specs=any_spec,
            scratch_shapes=[pltpu.SemaphoreType.DMA(())],
        ),
        out_shape=jax.ShapeDtypeStruct(xb0.shape, h2.dtype),
        input_output_aliases={2: 0},
        compiler_params=pltpu.CompilerParams(dimension_semantics=("arbitrary",)),
    )(dest_flat, h2, xb0)


def _experts_kernel(be_ref, nused_ref, xb_ref, w1_ref, b1_ref, w2_ref, b2_ref,
                    y_ref, w1b_ref, w2b_ref):
    i = pl.program_id(0)
    d_ff = w2_ref.shape[1]

    @pl.when((i == 0) | (be_ref[i] != be_ref[jnp.maximum(i - 1, 0)]))
    def _():
        w1b_ref[...] = w1_ref[0].astype(BF16)
        w2b_ref[...] = w2_ref[0].astype(BF16)

    @pl.when(i < nused_ref[0])
    def _():
        x = _load_row_tiles(xb_ref, EXPERT_TILE).astype(BF16)
        hh = _dot(x, w1b_ref[...]) + b1_ref[0]
        glu = jnp.minimum(hh[:, :d_ff], SWIGLU_LIMIT)
        lin = jnp.clip(hh[:, d_ff:], -SWIGLU_LIMIT, SWIGLU_LIMIT)
        act = glu * (1.0 / (1.0 + jnp.exp(-SWIGLU_ALPHA * glu))) * (lin + 1.0)
        _store_row_tiles(y_ref, _dot(act.astype(BF16), w2b_ref[...]) + b2_ref[0])

    @pl.when(i >= nused_ref[0])
    def _():
        y_ref[...] = jnp.zeros_like(y_ref)


def _experts(block_e, nused, xb, w1, b1, w2, b2):
    n_e, d, two_f = w1.shape
    d_ff = two_f // 2
    n_blocks = xb.shape[0] // (EXPERT_TILE * SUBLANES)
    row_block = pl.BlockSpec((EXPERT_TILE * SUBLANES, LANES), lambda i, be, nu: (i, 0))
    return pl.pallas_call(
        _experts_kernel,
        name="experts",
        grid_spec=pltpu.PrefetchScalarGridSpec(
            num_scalar_prefetch=2,
            grid=(n_blocks,),
            in_specs=[
                row_block,
                pl.BlockSpec((1, d, two_f), lambda i, be, nu: (be[i], 0, 0)),
                pl.BlockSpec((1, 1, two_f), lambda i, be, nu: (be[i], 0, 0)),
                pl.BlockSpec((1, d_ff, d), lambda i, be, nu: (be[i], 0, 0)),
                pl.BlockSpec((1, 1, d), lambda i, be, nu: (be[i], 0, 0)),
            ],
            out_specs=row_block,
            scratch_shapes=[pltpu.VMEM((d, two_f), BF16), pltpu.VMEM((d_ff, d), BF16)],
        ),
        out_shape=jax.ShapeDtypeStruct(xb.shape, F32),
        compiler_params=pltpu.CompilerParams(
            dimension_semantics=("arbitrary",),
            vmem_limit_bytes=VMEM_LIMIT),
    )(block_e, nused, xb, w1, b1.reshape(n_e, 1, two_f), w2, b2.reshape(n_e, 1, d))


def _combine_kernel(dest_ref, x1_ref, w_ref, gf_ref, yb_ref, out_ref, buf_ref, sem,
                    *, n_tok):
    tile = x1_ref.shape[0]
    step = pl.program_id(0)
    slot = step % 2

    def gather(to_step, to_slot):
        def issue(t, _):
            for k in range(TOP_K):
                d = dest_ref[k * n_tok + to_step * tile + t]
                pltpu.make_async_copy(
                    yb_ref.at[pl.ds(pl.multiple_of(d * SUBLANES, SUBLANES), SUBLANES), :],
                    buf_ref.at[to_slot, pl.ds(pl.multiple_of((k * tile + t) * SUBLANES,
                                                              SUBLANES), SUBLANES), :],
                    sem.at[to_slot]).start(priority=k % 2)
            return 0

        lax.fori_loop(0, tile, issue, 0, unroll=8)

    @pl.when(step == 0)
    def _():
        gather(0, 0)

    @pl.when(step + 1 < pl.num_programs(0))
    def _():
        gather(step + 1, 1 - slot)

    pltpu.make_async_copy(yb_ref.at[pl.ds(0, TOP_K * tile * SUBLANES), :],
                          buf_ref.at[slot], sem.at[slot]).wait()
    acc = x1_ref[...]
    w = w_ref[...]
    rows = buf_ref.at[slot]
    for k in range(TOP_K):
        acc = acc + _load_row_tiles(rows, tile, k * tile * SUBLANES) * w[:, k:k + 1]
    out_ref[...] = _rms(acc, gf_ref[...])


def _combine(dest_flat, x1, w_tok, gf, yb):
    n, d = x1.shape
    tile = min(COMBINE_TILE, n)
    return pl.pallas_call(
        functools.partial(_combine_kernel, n_tok=n),
        name="combine",
        grid_spec=pltpu.PrefetchScalarGridSpec(
            num_scalar_prefetch=1,
            grid=(n // tile,),
            in_specs=[
                pl.BlockSpec((tile, d), lambda i, dr: (i, 0)),
                pl.BlockSpec((tile, TOP_K), lambda i, dr: (i, 0)),
                pl.BlockSpec((1, d), lambda i, dr: (0, 0)),
                pl.BlockSpec(memory_space=pl.ANY),
            ],
            out_specs=pl.BlockSpec((tile, d), lambda i, dr: (i, 0)),
            scratch_shapes=[pltpu.VMEM((2, TOP_K * tile * SUBLANES, LANES), F32),
                            pltpu.SemaphoreType.DMA((2,))],
        ),
        out_shape=jax.ShapeDtypeStruct((n, d), F32),
        compiler_params=pltpu.CompilerParams(
            dimension_semantics=("arbitrary",),
            vmem_limit_bytes=VMEM_LIMIT),
    )(dest_flat, x1, w_tok, gf, yb)


def kernel(x, norm1_g, w_in, b_f, g_fox, g_sb, w_out, norm2_g, w_router, b_router,
           w1, b1, w2, b2, norm_f_g):
    b, s, d = x.shape
    n = b * s
    depth = norm1_g.shape[0]
    scale = HEAD_DIM ** -0.5 * LOG2E
    fw, sw = FOX_WIDTH, SB_WIDTH
    n_blocks = -(-n * TOP_K // EXPERT_TILE) + N_EXPERTS
    n_rows = n_blocks * EXPERT_TILE
    assert depth == 1, "the combine kernel fuses the final norm into the only layer"
    assert d == SUBLANES * LANES, "row-tile layout holds one model row per (8, 128) tile"
    wl = w_in[0]
    f0 = 3 * fw
    s0 = f0 + N_FOX_HEADS
    w_qk = jnp.concatenate(
        [wl[:, :fw] * scale, wl[:, s0:s0 + sw] * scale,
         wl[:, fw:2 * fw], wl[:, s0 + sw:s0 + 2 * sw]], axis=1).astype(BF16)
    w_vt = jnp.concatenate([wl[:, 2 * fw:f0], wl[:, s0 + 2 * sw:]], axis=1).T.astype(BF16)
    w_f = jnp.pad(wl[:, f0:s0], ((0, 0), (0, LANES - N_FOX_HEADS))).astype(BF16)
    bias_f = jnp.pad(b_f[0], (0, LANES - N_FOX_HEADS))[None, :]
    q, kf, ks, aug, vt = _in_proj(x, norm1_g[0][None, :], w_qk, w_vt, w_f, bias_f,
                                  _aug_selector())
    fox = _fox(q, kf, aug, vt)
    sb = _stick(q, ks, vt)
    x1, h2, e_k, w_k, rank_k, counts = _out_proj(
        x.reshape(n, d), fox.reshape(n, fw), sb.reshape(n, sw),
        g_fox[0][None, :], g_sb[0][None, :], w_out[0].astype(BF16),
        norm2_g[0][None, :], w_router[0].T, b_router[0][:, None])
    dest, block_e, nused = _route(counts[:, 0].astype(jnp.int32), e_k, rank_k, n_blocks)
    dest_flat = dest.reshape(-1)
    xb = _dispatch(dest_flat, h2, n_rows)
    yb = _experts(block_e, nused, xb, w1[0], b1[0], w2[0], b2[0])
    out = _combine(dest_flat, x1, w_k.T, norm_f_g[None, :], yb)
    return out.reshape(b, s, d)
```

```python
import functools

import jax
import jax.numpy as jnp
import numpy as np
from jax import lax
from jax.experimental import pallas as pl
from jax.experimental.pallas import tpu as pltpu

HEAD_DIM = 64
N_FOX_HEADS = 8
N_SB_HEADS = 8
FOX_WIDTH = N_FOX_HEADS * HEAD_DIM
SB_WIDTH = N_SB_HEADS * HEAD_DIM
N_EXPERTS = 32
TOP_K = 4
SWIGLU_LIMIT = 7.0
SWIGLU_ALPHA = 1.702
RMS_EPS = 1e-5
LOG2E = 1.4426950408889634

LANES = 128
SUBLANES = 8
PAIRS = FOX_WIDTH // LANES

ROW_TILE = 256
OUT_TILE = 1024
Q_TILE = 512
K_TILE = 512
CUM_CHUNK = 256
ROW_CHUNK = 64
EXPERT_TILE = 256
DISPATCH_TILE = 512
COMBINE_TILE = 256
VMEM_LIMIT = 56 * 1024 * 1024

F32 = jnp.float32
BF16 = jnp.bfloat16


def _dot(a, b):
    return jnp.dot(a, b, preferred_element_type=F32)


def _dot_nt(a, b):
    return lax.dot_general(a, b, (((1,), (1,)), ((), ())), preferred_element_type=F32)


def _split2(v):
    hi = v.astype(BF16)
    lo = (v - hi.astype(F32)).astype(BF16)
    return hi, lo


def _split3(v):
    hi = v.astype(BF16)
    r = v - hi.astype(F32)
    mid = r.astype(BF16)
    lo = (r - mid.astype(F32)).astype(BF16)
    return hi, mid, lo


def _softplus(z):
    return jnp.maximum(z, 0.0) + jnp.log1p(jnp.exp(-jnp.abs(z)))


def _softplus2(z2):
    sign_bit = jnp.uint32(0x80000000)
    neg_abs = lax.bitcast_convert_type(lax.bitcast_convert_type(z2, jnp.uint32) | sign_bit, F32)
    return jnp.maximum(z2, 0.0) + jnp.log2(1.0 + jnp.exp2(neg_abs))


def _load_row_tiles(ref, rows, first=0):
    return jnp.concatenate(
        [ref[pl.ds(first + c, rows, stride=SUBLANES), :] for c in range(SUBLANES)], axis=1)


def _store_row_tiles(ref, value):
    rows = value.shape[0]
    for c in range(SUBLANES):
        ref[pl.ds(c, rows, stride=SUBLANES), :] = value[:, c * LANES:(c + 1) * LANES]


def _rms(v, g):
    return v * lax.rsqrt(jnp.mean(v * v, axis=-1, keepdims=True) + RMS_EPS) * g


def _in_proj_kernel(x_ref, g_ref, wqk_ref, wvt_ref, wf_ref, bf_ref, sel_ref,
                    q_ref, kf_ref, ks_ref, aug_ref, vt_ref, carry_ref):
    @pl.when(pl.program_id(1) == 0)
    def _():
        carry_ref[...] = jnp.zeros_like(carry_ref)

    tm = x_ref.shape[1]
    mix = FOX_WIDTH + SB_WIDTH
    hb = _rms(x_ref[0], g_ref[...]).astype(BF16)
    qk = _dot(hb, wqk_ref[...])
    q_ref[0] = qk[:, :mix].astype(BF16)
    kf_ref[0] = qk[:, mix:mix + FOX_WIDTH].astype(BF16)
    ks_ref[0] = qk[:, mix + FOX_WIDTH:].astype(BF16)
    vt_ref[0] = _dot_nt(wvt_ref[...], hb).astype(BF16)
    log_f = -LOG2E * _softplus(-(_dot(hb, wf_ref[...]) + bf_ref[...]))
    row = lax.broadcasted_iota(jnp.int32, (tm, tm), 0)
    col = lax.broadcasted_iota(jnp.int32, (tm, tm), 1)
    tri = (col <= row).astype(BF16)
    hi, mid, lo = _split3(log_f)
    c = _dot(tri, hi) + _dot(tri, mid) + _dot(tri, lo) + carry_ref[...]
    carry_ref[...] = c[tm - 1:tm, :]
    c_hi, c_mid, c_lo = _split3(c)
    aug = _dot(c_hi, sel_ref[0]) + _dot(c_mid, sel_ref[1]) + _dot(c_lo, sel_ref[2])
    aug_ref[0] = aug.astype(BF16)


def _aug_selector():
    sel = np.zeros((3, LANES, FOX_WIDTH), np.float32)
    for t in range(3):
        for h in range(N_FOX_HEADS):
            sel[t, h, LANES * (h // 2) + 3 * (h % 2) + t] = -1.0
    return jnp.asarray(sel, BF16)


def _in_proj(x, g, wqk, wvt, wf, bf, sel):
    b, s, d = x.shape
    mix = FOX_WIDTH + SB_WIDTH
    tm = min(ROW_TILE, s)
    const2 = lambda i, j: (0, 0)
    return pl.pallas_call(
        _in_proj_kernel,
        name="in_proj",
        grid=(b, s // tm),
        in_specs=[
            pl.BlockSpec((1, tm, d), lambda i, j: (i, j, 0)),
            pl.BlockSpec((1, d), const2),
            pl.BlockSpec((d, 2 * mix), const2),
            pl.BlockSpec((mix, d), const2),
            pl.BlockSpec((d, LANES), const2),
            pl.BlockSpec((1, LANES), const2),
            pl.BlockSpec((3, LANES, FOX_WIDTH), lambda i, j: (0, 0, 0)),
        ],
        out_specs=[
            pl.BlockSpec((1, tm, mix), lambda i, j: (i, j, 0)),
            pl.BlockSpec((1, tm, FOX_WIDTH), lambda i, j: (i, j, 0)),
            pl.BlockSpec((1, tm, SB_WIDTH), lambda i, j: (i, j, 0)),
            pl.BlockSpec((1, tm, FOX_WIDTH), lambda i, j: (i, j, 0)),
            pl.BlockSpec((1, mix, tm), lambda i, j: (i, 0, j)),
        ],
        out_shape=[
            jax.ShapeDtypeStruct((b, s, mix), BF16),
            jax.ShapeDtypeStruct((b, s, FOX_WIDTH), BF16),
            jax.ShapeDtypeStruct((b, s, SB_WIDTH), BF16),
            jax.ShapeDtypeStruct((b, s, FOX_WIDTH), BF16),
            jax.ShapeDtypeStruct((b, mix, s), BF16),
        ],
        scratch_shapes=[pltpu.VMEM((1, LANES), F32)],
        compiler_params=pltpu.CompilerParams(
            dimension_semantics=("parallel", "arbitrary"),
            vmem_limit_bytes=VMEM_LIMIT),
    )(x, g, wqk, wvt, wf, bf, sel)


def _stack_heads(q, ones_cols):
    tq = q.shape[0]
    lane = lax.broadcasted_iota(jnp.int32, (tq, LANES), 1)
    low = lane < HEAD_DIM
    qf = q.astype(F32)
    halves = [jnp.where(low, qf, 0.0), jnp.where(low, 0.0, qf)]
    if ones_cols:
        halves = [jnp.concatenate([halves[hh], jnp.where(
            (lane >= 3 * hh) & (lane < 3 * hh + 3), 1.0, 0.0)], axis=1) for hh in range(2)]
    return jnp.concatenate(halves, axis=0).astype(q.dtype)


def _unstack_heads(acc_t, tq):
    sub = lax.broadcasted_iota(jnp.int32, (LANES, tq), 0)
    return jnp.where(sub < HEAD_DIM, acc_t[:, :tq], acc_t[:, tq:]).T


def _fox_kernel(q_ref, k_ref, aug_ref, vt_ref, o_ref,
                q2_ref, m_ref, l_ref, acc_ref, s_ref, p_ref, a_ref):
    i = pl.program_id(2)
    tq = q_ref.shape[1]
    q2_ref[...] = _stack_heads(q_ref[0], True)
    m_ref[...] = jnp.full(m_ref.shape, -1e30, F32)
    l_ref[...] = jnp.zeros_like(l_ref)
    acc_ref[...] = jnp.zeros_like(acc_ref)

    def scores(j, buf):
        k0 = pl.multiple_of(j * tq, tq)
        ka = jnp.concatenate(
            [k_ref[0, pl.ds(k0, tq), :], aug_ref[0, pl.ds(k0, tq), :]], axis=1)
        s_ref[buf] = _dot_nt(ka, q2_ref[...])

    def softmax_update(buf, diagonal):
        if diagonal:
            key = lax.broadcasted_iota(jnp.int32, (tq, 2 * tq), 0)
            qry = lax.broadcasted_iota(jnp.int32, (tq, 2 * tq), 1)
            qry = jnp.where(qry >= tq, qry - tq, qry)
            s_ref[buf] = jnp.where(key <= qry, s_ref[buf], -jnp.inf)
        m_old = m_ref[...]
        m_new = jnp.maximum(m_old, jnp.max(s_ref[buf], axis=0, keepdims=True))
        pr = jnp.exp2(s_ref[buf] - m_new)
        alpha = jnp.exp2(m_old - m_new)
        m_ref[...] = m_new
        l_ref[...] = l_ref[...] * alpha + jnp.sum(pr, axis=0, keepdims=True)
        p_ref[buf] = pr.astype(BF16)
        a_ref[buf] = alpha

    def accumulate(j, buf):
        k0 = pl.multiple_of(j * tq, tq)
        acc_ref[...] = (acc_ref[...] * a_ref[buf]
                        + _dot(vt_ref[0, :, pl.ds(k0, tq)], p_ref[buf]))

    def step(j, buf):
        scores(j + 1, 1 - buf)
        softmax_update(buf, False)
        accumulate(jnp.maximum(j - 1, 0), 1 - buf)

    def last(buf):
        softmax_update(buf, True)
        accumulate(jnp.maximum(i - 1, 0), 1 - buf)
        accumulate(i, buf)

    scores(0, 0)
    p_ref[1] = jnp.zeros(p_ref.shape[1:], BF16)
    a_ref[1] = jnp.ones(a_ref.shape[1:], F32)

    def two_steps(t, _):
        step(2 * t, 0)
        step(2 * t + 1, 1)
        return 0

    lax.fori_loop(0, i // 2, two_steps, 0)

    @pl.when(i % 2 == 1)
    def _():
        step(i - 1, 0)
        last(1)

    @pl.when(i % 2 == 0)
    def _():
        last(0)

    o_ref[0] = _unstack_heads(acc_ref[...] / l_ref[...], tq)


def _fox(q, kf, aug, vt):
    b, s, _ = q.shape
    tq = min(Q_TILE, s)
    return pl.pallas_call(
        _fox_kernel,
        name="fox",
        grid=(b, PAIRS, s // tq),
        in_specs=[
            pl.BlockSpec((1, tq, LANES), lambda bi, p, i: (bi, i, p)),
            pl.BlockSpec((1, s, LANES), lambda bi, p, i: (bi, 0, p)),
            pl.BlockSpec((1, s, LANES), lambda bi, p, i: (bi, 0, p)),
            pl.BlockSpec((1, LANES, s), lambda bi, p, i: (bi, p, 0)),
        ],
        out_specs=pl.BlockSpec((1, tq, LANES), lambda bi, p, i: (bi, i, p)),
        out_shape=jax.ShapeDtypeStruct((b, s, FOX_WIDTH), F32),
        scratch_shapes=[
            pltpu.VMEM((2 * tq, 2 * LANES), BF16),
            pltpu.VMEM((1, 2 * tq), F32),
            pltpu.VMEM((1, 2 * tq), F32),
            pltpu.VMEM((LANES, 2 * tq), F32),
            pltpu.VMEM((2, tq, 2 * tq), F32),
            pltpu.VMEM((2, tq, 2 * tq), BF16),
            pltpu.VMEM((2, 1, 2 * tq), F32),
        ],
        compiler_params=pltpu.CompilerParams(
            dimension_semantics=("parallel", "parallel", "parallel"),
            vmem_limit_bytes=VMEM_LIMIT),
    )(q, kf, aug, vt)


def _stick_kernel(q_ref, k_ref, vt_ref, o_ref, q2_ref, tail_ref, acc_ref, z_ref, a_ref):
    i = pl.program_id(2)
    tq = q_ref.shape[1]
    tc = min(CUM_CHUNK, tq)
    q2_ref[...] = _stack_heads(q_ref[0], False)
    tail_ref[...] = jnp.zeros_like(tail_ref)
    acc_ref[...] = jnp.zeros_like(acc_ref)
    kr = lax.broadcasted_iota(jnp.int32, (tc, tc), 0)
    kc = lax.broadcasted_iota(jnp.int32, (tc, tc), 1)
    later = (kc > kr).astype(BF16)

    def scores(j, buf):
        k0 = pl.multiple_of(j * tq, tq)
        z_ref[buf] = _dot_nt(k_ref[0, pl.ds(k0, tq), :], q2_ref[...])

    def weights(buf, diagonal):
        tail = tail_ref[...]
        for ch in reversed(range(tq // tc)):
            z_c = z_ref[buf, ch * tc:(ch + 1) * tc, :]
            sp_c = _softplus2(z_c)
            if diagonal:
                key = lax.broadcasted_iota(jnp.int32, (tc, 2 * tq), 0) + ch * tc
                qry = lax.broadcasted_iota(jnp.int32, (tc, 2 * tq), 1)
                qry = jnp.where(qry >= tq, qry - tq, qry)
                strict = key < qry
                sp_m = jnp.where(strict, sp_c, 0.0)
            else:
                sp_m = sp_c
            suffix = _dot(later, sp_m.astype(BF16))
            a = jnp.exp2((z_c - sp_c) - (suffix + tail))
            if diagonal:
                a = jnp.where(strict, a, 0.0)
            a_ref[buf, ch * tc:(ch + 1) * tc, :] = a.astype(BF16)
            tail = tail + suffix[0:1, :] + sp_m[0:1, :]
        tail_ref[...] = tail

    def accumulate(j, buf):
        k0 = pl.multiple_of(j * tq, tq)
        acc_ref[...] = acc_ref[...] + _dot(vt_ref[0, :, pl.ds(k0, tq)], a_ref[buf])

    def step(n, buf, diagonal):
        j = i - n
        scores(j - 1, 1 - buf)
        weights(buf, diagonal)
        if not diagonal:
            accumulate(j + 1, 1 - buf)

    def last(buf, diagonal):
        weights(buf, diagonal)
        if not diagonal:
            accumulate(1, 1 - buf)
        accumulate(0, buf)

    scores(i, 0)

    @pl.when(i == 0)
    def _():
        last(0, True)

    @pl.when(i > 0)
    def _():
        step(0, 0, True)

        def two_steps(t, _):
            step(1 + 2 * t, 1, False)
            step(2 + 2 * t, 0, False)
            return 0

        lax.fori_loop(0, (i - 1) // 2, two_steps, 0)

        @pl.when(i % 2 == 0)
        def _():
            step(i - 1, 1, False)
            last(0, False)

        @pl.when(i % 2 == 1)
        def _():
            last(1, False)

    o_ref[0] = _unstack_heads(acc_ref[...], tq)


def _stick(q, ks, vt):
    b, s, _ = q.shape
    tq = min(Q_TILE, s)
    return pl.pallas_call(
        _stick_kernel,
        name="stick",
        grid=(b, PAIRS, s // tq),
        in_specs=[
            pl.BlockSpec((1, tq, LANES), lambda bi, p, i: (bi, i, PAIRS + p)),
            pl.BlockSpec((1, s, LANES), lambda bi, p, i: (bi, 0, p)),
            pl.BlockSpec((1, LANES, s), lambda bi, p, i: (bi, PAIRS + p, 0)),
        ],
        out_specs=pl.BlockSpec((1, tq, LANES), lambda bi, p, i: (bi, i, p)),
        out_shape=jax.ShapeDtypeStruct((b, s, SB_WIDTH), F32),
        scratch_shapes=[
            pltpu.VMEM((2 * tq, LANES), BF16),
            pltpu.VMEM((1, 2 * tq), F32),
            pltpu.VMEM((LANES, 2 * tq), F32),
            pltpu.VMEM((2, tq, 2 * tq), F32),
            pltpu.VMEM((2, tq, 2 * tq), BF16),
        ],
        compiler_params=pltpu.CompilerParams(
            dimension_semantics=("parallel", "parallel", "parallel"),
            vmem_limit_bytes=VMEM_LIMIT),
    )(q, ks, vt)


def _out_proj_kernel(x_ref, fox_ref, sb_ref, gfox_ref, gsb_ref, wo_ref, g2_ref,
                     wr_ref, br_ref,
                     x1_ref, h2_ref, e_ref, w_ref, rank_ref, cnt_ref, carry_ref):
    @pl.when(pl.program_id(0) == 0)
    def _():
        carry_ref[...] = jnp.zeros_like(carry_ref)

    tm = x_ref.shape[0]
    a = _rms(fox_ref[...], gfox_ref[...]).astype(BF16)
    bb = _rms(sb_ref[...], gsb_ref[...]).astype(BF16)
    x1 = x_ref[...] + _dot(a, wo_ref[0:FOX_WIDTH, :]) + _dot(bb, wo_ref[FOX_WIDTH:, :])
    x1_ref[...] = x1
    h2 = _rms(x1, g2_ref[...])
    _store_row_tiles(h2_ref, h2)
    h_hi, h_lo = _split2(h2)
    w_hi, w_lo = _split2(wr_ref[...])
    logits = (_dot_nt(w_hi, h_hi) + _dot_nt(w_hi, h_lo) + _dot_nt(w_lo, h_hi)
              + br_ref[...])
    e_iota = lax.broadcasted_iota(jnp.int32, (N_EXPERTS, tm), 0)
    work = logits
    vals, idxs, sels = [], [], []
    for _ in range(TOP_K):
        m = jnp.max(work, axis=0, keepdims=True)
        idx = jnp.min(jnp.where(work == m, e_iota, N_EXPERTS), axis=0, keepdims=True)
        sel = e_iota == idx
        work = jnp.where(sel, -jnp.inf, work)
        vals.append(m)
        idxs.append(idx)
        sels.append(sel)
    exps = [jnp.exp(v - vals[0]) for v in vals]
    denom = exps[0] + exps[1] + exps[2] + exps[3]
    e_ref[...] = jnp.concatenate(idxs, axis=0)
    w_ref[...] = jnp.concatenate([ex / denom for ex in exps], axis=0)
    chosen = jnp.where(sels[0] | sels[1] | sels[2] | sels[3], 1.0, 0.0)
    row = lax.broadcasted_iota(jnp.int32, (tm, tm), 0)
    col = lax.broadcasted_iota(jnp.int32, (tm, tm), 1)
    before = (row < col).astype(BF16)
    rank = _dot(chosen.astype(BF16), before) + carry_ref[...]
    rank_ref[...] = jnp.concatenate(
        [jnp.sum(jnp.where(sel, rank, 0.0), axis=0, keepdims=True) for sel in sels],
        axis=0).astype(jnp.int32)
    total = carry_ref[...] + jnp.sum(chosen, axis=1, keepdims=True)
    carry_ref[...] = total
    cnt_ref[...] = jnp.broadcast_to(total, cnt_ref.shape)


def _out_proj(x2, fox2, sb2, gfox, gsb, wo, g2, wr_t, br):
    n, d = x2.shape
    tm = min(OUT_TILE, n)
    const = lambda i: (0, 0)
    return pl.pallas_call(
        _out_proj_kernel,
        name="out_proj",
        grid=(n // tm,),
        in_specs=[
            pl.BlockSpec((tm, d), lambda i: (i, 0)),
            pl.BlockSpec((tm, FOX_WIDTH), lambda i: (i, 0)),
            pl.BlockSpec((tm, SB_WIDTH), lambda i: (i, 0)),
            pl.BlockSpec((1, FOX_WIDTH), const),
            pl.BlockSpec((1, SB_WIDTH), const),
            pl.BlockSpec((FOX_WIDTH + SB_WIDTH, d), const),
            pl.BlockSpec((1, d), const),
            pl.BlockSpec((N_EXPERTS, d), const),
            pl.BlockSpec((N_EXPERTS, 1), const),
        ],
        out_specs=[
            pl.BlockSpec((tm, d), lambda i: (i, 0)),
            pl.BlockSpec((tm * SUBLANES, LANES), lambda i: (i, 0)),
            pl.BlockSpec((TOP_K, tm), lambda i: (0, i)),
            pl.BlockSpec((TOP_K, tm), lambda i: (0, i)),
            pl.BlockSpec((TOP_K, tm), lambda i: (0, i)),
            pl.BlockSpec((N_EXPERTS, LANES), const),
        ],
        out_shape=[
            jax.ShapeDtypeStruct((n, d), F32),
            jax.ShapeDtypeStruct((n * SUBLANES, LANES), F32),
            jax.ShapeDtypeStruct((TOP_K, n), jnp.int32),
            jax.ShapeDtypeStruct((TOP_K, n), F32),
            jax.ShapeDtypeStruct((TOP_K, n), jnp.int32),
            jax.ShapeDtypeStruct((N_EXPERTS, LANES), F32),
        ],
        scratch_shapes=[pltpu.VMEM((N_EXPERTS, 1), F32)],
        compiler_params=pltpu.CompilerParams(
            dimension_semantics=("arbitrary",),
            vmem_limit_bytes=VMEM_LIMIT),
    )(x2, fox2, sb2, gfox, gsb, wo, g2, wr_t, br)


def _route_kernel(cnt_ref, e_ref, rank_ref, dest_ref, start_ref):
    def per_expert(e, pstart):
        start_ref[e] = pstart
        return pstart + (cnt_ref[e] + EXPERT_TILE - 1) // EXPERT_TILE * EXPERT_TILE

    lax.fori_loop(0, N_EXPERTS, per_expert, 0)
    ev = e_ref[...]
    offs = jnp.zeros(ev.shape, jnp.int32)
    for e in range(N_EXPERTS):
        offs = jnp.where(ev == e, start_ref[e], offs)
    dest_ref[...] = rank_ref[...] + offs


def _route(counts, e_k, rank_k):
    smem = pl.BlockSpec(memory_space=pltpu.SMEM)
    vmem = pl.BlockSpec(memory_space=pltpu.VMEM)
    return pl.pallas_call(
        _route_kernel,
        name="route",
        in_specs=[smem, vmem, vmem],
        out_specs=[vmem, smem],
        out_shape=[
            jax.ShapeDtypeStruct(e_k.shape, jnp.int32),
            jax.ShapeDtypeStruct((N_EXPERTS,), jnp.int32),
        ],
    )(counts, e_k, rank_k)


def _zero_unused_blocks(dst_ref, zero_block_ref, sem, start_ref, cnt_ref):
    block_rows = zero_block_ref.shape[0]
    last = N_EXPERTS - 1
    used = (start_ref[last] + (cnt_ref[last] + EXPERT_TILE - 1) // EXPERT_TILE
            * EXPERT_TILE) * SUBLANES
    count = (dst_ref.shape[0] - used) // block_rows

    def copy(b):
        rows = pl.ds(pl.multiple_of(used + b * block_rows, block_rows), block_rows)
        return pltpu.make_async_copy(zero_block_ref, dst_ref.at[rows, :], sem)

    def start(b, _):
        copy(b).start()
        return 0

    def wait(b, _):
        copy(b).wait()
        return 0

    lax.fori_loop(0, count, start, 0)
    lax.fori_loop(0, count, wait, 0)


def _dispatch_kernel(dest_ref, start_ref, cnt_ref, h_ref, xb_ref, zero_ref, sem, pad_sem,
                     *, n_tok, tile):
    base = pl.program_id(0) * tile

    @pl.when(pl.program_id(0) == 0)
    def _():
        zero_ref[...] = jnp.zeros_like(zero_ref)
        zero_row = zero_ref.at[pl.ds(0, SUBLANES), :]

        def pad_rows(e, wait):
            first = start_ref[e] + cnt_ref[e]

            def one(r, _):
                row = pl.multiple_of((first + r) * SUBLANES, SUBLANES)
                copy = pltpu.make_async_copy(
                    zero_row, xb_ref.at[pl.ds(row, SUBLANES), :], pad_sem)
                if wait:
                    copy.wait()
                else:
                    copy.start()
                return 0

            return lax.fori_loop(0, (-cnt_ref[e]) % EXPERT_TILE, one, 0)

        lax.fori_loop(0, N_EXPERTS, lambda e, _: pad_rows(e, False), 0)
        lax.fori_loop(0, N_EXPERTS, lambda e, _: pad_rows(e, True), 0)
        _zero_unused_blocks(xb_ref, zero_ref, pad_sem, start_ref, cnt_ref)

    def issue(t, _):
        src = h_ref.at[pl.ds(pl.multiple_of(t * SUBLANES, SUBLANES), SUBLANES), :]
        for k in range(TOP_K):
            d = dest_ref[k * n_tok + base + t]
            pltpu.make_async_copy(
                src, xb_ref.at[pl.ds(pl.multiple_of(d * SUBLANES, SUBLANES), SUBLANES), :],
                sem).start(priority=k % 2)
        return 0

    lax.fori_loop(0, tile, issue, 0, unroll=8)
    rows = xb_ref.at[pl.ds(0, TOP_K * tile * SUBLANES), :]
    pltpu.make_async_copy(rows, rows, sem).wait()


def _dispatch(dest_flat, start, counts, h2, n_rows):
    n = h2.shape[0] // SUBLANES
    tile = min(DISPATCH_TILE, n)
    return pl.pallas_call(
        functools.partial(_dispatch_kernel, n_tok=n, tile=tile),
        name="dispatch",
        grid_spec=pltpu.PrefetchScalarGridSpec(
            num_scalar_prefetch=3,
            grid=(n // tile,),
            in_specs=[pl.BlockSpec((tile * SUBLANES, LANES), lambda i, dr, st, ct: (i, 0))],
            out_specs=pl.BlockSpec(memory_space=pl.ANY),
            scratch_shapes=[pltpu.VMEM((EXPERT_TILE * SUBLANES, LANES), F32),
                            pltpu.SemaphoreType.DMA(()),
                            pltpu.SemaphoreType.DMA(())],
        ),
        out_shape=jax.ShapeDtypeStruct((n_rows * SUBLANES, LANES), h2.dtype),
        compiler_params=pltpu.CompilerParams(dimension_semantics=("arbitrary",)),
    )(dest_flat, start, counts, h2)


def _experts_kernel(start_ref, cnt_ref, xb_ref, w1_ref, b1_ref, w2_ref, b2_ref, y_ref,
                    w1b_ref, w2b_ref, xbuf_ref, ybuf_ref, sem_in, sem_out):
    e = pl.program_id(0)
    d_ff = w2_ref.shape[1]
    block_rows = EXPERT_TILE * SUBLANES
    n_blk = (cnt_ref[e] + EXPERT_TILE - 1) // EXPERT_TILE
    first = start_ref[e] * SUBLANES
    w1b_ref[...] = w1_ref[0].astype(BF16)
    w2b_ref[...] = w2_ref[0].astype(BF16)

    def rows(j):
        return pl.ds(pl.multiple_of(first + j * block_rows, block_rows), block_rows)

    def copy_in(j, slot):
        return pltpu.make_async_copy(xb_ref.at[rows(j), :], xbuf_ref.at[slot], sem_in.at[slot])

    def copy_out(j, slot):
        return pltpu.make_async_copy(ybuf_ref.at[slot], y_ref.at[rows(j), :], sem_out.at[slot])

    @pl.when(n_blk > 0)
    def _():
        copy_in(0, 0).start()

    def block(j, _):
        slot = j % 2

        @pl.when(j + 1 < n_blk)
        def _():
            copy_in(j + 1, 1 - slot).start()

        copy_in(j, slot).wait()

        @pl.when(j >= 2)
        def _():
            copy_out(j - 2, slot).wait()

        x = _load_row_tiles(xbuf_ref.at[slot], EXPERT_TILE).astype(BF16)
        hh = _dot(x, w1b_ref[...]) + b1_ref[0]
        glu = jnp.minimum(hh[:, :d_ff], SWIGLU_LIMIT)
        lin = jnp.clip(hh[:, d_ff:], -SWIGLU_LIMIT, SWIGLU_LIMIT)
        act = glu * (1.0 / (1.0 + jnp.exp(-SWIGLU_ALPHA * glu))) * (lin + 1.0)
        _store_row_tiles(ybuf_ref.at[slot], _dot(act.astype(BF16), w2b_ref[...]) + b2_ref[0])
        copy_out(j, slot).start()
        return 0

    lax.fori_loop(0, n_blk, block, 0)

    for back in (1, 2):
        @pl.when(n_blk >= back)
        def _():
            copy_out(n_blk - back, (n_blk - back) % 2).wait()

    @pl.when(e == pl.num_programs(0) - 1)
    def _():
        ybuf_ref[0] = jnp.zeros(ybuf_ref.shape[1:], F32)
        _zero_unused_blocks(y_ref, ybuf_ref.at[0], sem_out.at[0], start_ref, cnt_ref)


def _experts(start, counts, xb, w1, b1, w2, b2):
    n_e, d, two_f = w1.shape
    d_ff = two_f // 2
    block_rows = EXPERT_TILE * SUBLANES
    any_spec = pl.BlockSpec(memory_space=pl.ANY)
    return pl.pallas_call(
        _experts_kernel,
        name="experts",
        grid_spec=pltpu.PrefetchScalarGridSpec(
            num_scalar_prefetch=2,
            grid=(n_e,),
            in_specs=[
                any_spec,
                pl.BlockSpec((1, d, two_f), lambda e, st, ct: (e, 0, 0)),
                pl.BlockSpec((1, 1, two_f), lambda e, st, ct: (e, 0, 0)),
                pl.BlockSpec((1, d_ff, d), lambda e, st, ct: (e, 0, 0)),
                pl.BlockSpec((1, 1, d), lambda e, st, ct: (e, 0, 0)),
            ],
            out_specs=any_spec,
            scratch_shapes=[
                pltpu.VMEM((d, two_f), BF16),
                pltpu.VMEM((d_ff, d), BF16),
                pltpu.VMEM((2, block_rows, LANES), F32),
                pltpu.VMEM((2, block_rows, LANES), F32),
                pltpu.SemaphoreType.DMA((2,)),
                pltpu.SemaphoreType.DMA((2,)),
            ],
        ),
        out_shape=jax.ShapeDtypeStruct(xb.shape, F32),
        compiler_params=pltpu.CompilerParams(
            dimension_semantics=("arbitrary",),
            vmem_limit_bytes=VMEM_LIMIT),
    )(start, counts, xb, w1, b1.reshape(n_e, 1, two_f), w2, b2.reshape(n_e, 1, d))


def _combine_kernel(dest_ref, x1_ref, w_ref, gf_ref, yb_ref, out_ref, buf_ref, sem,
                    *, n_tok):
    tile = x1_ref.shape[0]
    step = pl.program_id(0)
    slot = step % 2

    def gather(to_step, to_slot):
        def issue(t, _):
            for k in range(TOP_K):
                d = dest_ref[k * n_tok + to_step * tile + t]
                pltpu.make_async_copy(
                    yb_ref.at[pl.ds(pl.multiple_of(d * SUBLANES, SUBLANES), SUBLANES), :],
                    buf_ref.at[to_slot, pl.ds(pl.multiple_of((k * tile + t) * SUBLANES,
                                                              SUBLANES), SUBLANES), :],
                    sem.at[to_slot]).start(priority=k % 2)
            return 0

        lax.fori_loop(0, tile, issue, 0, unroll=8)

    @pl.when(step == 0)
    def _():
        gather(0, 0)

    @pl.when(step + 1 < pl.num_programs(0))
    def _():
        gather(step + 1, 1 - slot)

    pltpu.make_async_copy(yb_ref.at[pl.ds(0, TOP_K * tile * SUBLANES), :],
                          buf_ref.at[slot], sem.at[slot]).wait()
    acc = x1_ref[...]
    w = w_ref[...]
    rows = buf_ref.at[slot]
    for k in range(TOP_K):
        acc = acc + _load_row_tiles(rows, tile, k * tile * SUBLANES) * w[:, k:k + 1]
    out_ref[...] = _rms(acc, gf_ref[...])


def _combine(dest_flat, x1, w_tok, gf, yb):
    n, d = x1.shape
    tile = min(COMBINE_TILE, n)
    return pl.pallas_call(
        functools.partial(_combine_kernel, n_tok=n),
        name="combine",
        grid_spec=pltpu.PrefetchScalarGridSpec(
            num_scalar_prefetch=1,
            grid=(n // tile,),
            in_specs=[
                pl.BlockSpec((tile, d), lambda i, dr: (i, 0)),
                pl.BlockSpec((tile, TOP_K), lambda i, dr: (i, 0)),
                pl.BlockSpec((1, d), lambda i, dr: (0, 0)),
                pl.BlockSpec(memory_space=pl.ANY),
            ],
            out_specs=pl.BlockSpec((tile, d), lambda i, dr: (i, 0)),
            scratch_shapes=[pltpu.VMEM((2, TOP_K * tile * SUBLANES, LANES), F32),
                            pltpu.SemaphoreType.DMA((2,))],
        ),
        out_shape=jax.ShapeDtypeStruct((n, d), F32),
        compiler_params=pltpu.CompilerParams(
            dimension_semantics=("arbitrary",),
            vmem_limit_bytes=VMEM_LIMIT),
    )(dest_flat, x1, w_tok, gf, yb)


def kernel(x, norm1_g, w_in, b_f, g_fox, g_sb, w_out, norm2_g, w_router, b_router,
           w1, b1, w2, b2, norm_f_g):
    b, s, d = x.shape
    n = b * s
    depth = norm1_g.shape[0]
    scale = HEAD_DIM ** -0.5 * LOG2E
    fw, sw = FOX_WIDTH, SB_WIDTH
    n_blocks = -(-n * TOP_K // EXPERT_TILE) + N_EXPERTS
    n_rows = n_blocks * EXPERT_TILE
    assert depth == 1, "the combine kernel fuses the final norm into the only layer"
    assert d == SUBLANES * LANES, "row-tile layout holds one model row per (8, 128) tile"
    wl = w_in[0]
    f0 = 3 * fw
    s0 = f0 + N_FOX_HEADS
    w_qk = jnp.concatenate(
        [wl[:, :fw] * scale, wl[:, s0:s0 + sw] * scale,
         wl[:, fw:2 * fw], wl[:, s0 + sw:s0 + 2 * sw]], axis=1).astype(BF16)
    w_vt = jnp.concatenate([wl[:, 2 * fw:f0], wl[:, s0 + 2 * sw:]], axis=1).T.astype(BF16)
    w_f = jnp.pad(wl[:, f0:s0], ((0, 0), (0, LANES - N_FOX_HEADS))).astype(BF16)
    bias_f = jnp.pad(b_f[0], (0, LANES - N_FOX_HEADS))[None, :]
    q, kf, ks, aug, vt = _in_proj(x, norm1_g[0][None, :], w_qk, w_vt, w_f, bias_f,
                                  _aug_selector())
    fox = _fox(q, kf, aug, vt)
    sb = _stick(q, ks, vt)
    x1, h2, e_k, w_k, rank_k, counts = _out_proj(
        x.reshape(n, d), fox.reshape(n, fw), sb.reshape(n, sw),
        g_fox[0][None, :], g_sb[0][None, :], w_out[0].astype(BF16),
        norm2_g[0][None, :], w_router[0].T, b_router[0][:, None])
    counts = counts[:, 0].astype(jnp.int32)
    dest, start = _route(counts, e_k, rank_k)
    dest_flat = dest.reshape(-1)
    xb = _dispatch(dest_flat, start, counts, h2, n_rows)
    yb = _experts(start, counts, xb, w1[0], b1[0], w2[0], b2[0])
    out = _combine(dest_flat, x1, w_k.T, norm_f_g[None, :], yb)
    return out.reshape(b, s, d)
```

```python
import functools

import jax
import jax.numpy as jnp
import numpy as np
from jax import lax
from jax.experimental import pallas as pl
from jax.experimental.pallas import tpu as pltpu

HEAD_DIM = 64
N_FOX_HEADS = 8
N_SB_HEADS = 8
FOX_WIDTH = N_FOX_HEADS * HEAD_DIM
SB_WIDTH = N_SB_HEADS * HEAD_DIM
N_EXPERTS = 32
TOP_K = 4
SWIGLU_LIMIT = 7.0
SWIGLU_ALPHA = 1.702
RMS_EPS = 1e-5
LOG2E = 1.4426950408889634

LANES = 128
SUBLANES = 8
PAIRS = FOX_WIDTH // LANES

ROW_TILE = 256
OUT_TILE = 1024
Q_TILE = 512
K_TILE = 512
CUM_CHUNK = 256
ROW_CHUNK = 64
EXPERT_TILE = 512
DISPATCH_TILE = 512
COMBINE_TILE = 256
VMEM_LIMIT = 56 * 1024 * 1024

F32 = jnp.float32
BF16 = jnp.bfloat16


def _dot(a, b):
    return jnp.dot(a, b, preferred_element_type=F32)


def _dot_nt(a, b):
    return lax.dot_general(a, b, (((1,), (1,)), ((), ())), preferred_element_type=F32)


def _split2(v):
    hi = v.astype(BF16)
    lo = (v - hi.astype(F32)).astype(BF16)
    return hi, lo


def _split3(v):
    hi = v.astype(BF16)
    r = v - hi.astype(F32)
    mid = r.astype(BF16)
    lo = (r - mid.astype(F32)).astype(BF16)
    return hi, mid, lo


def _softplus(z):
    return jnp.maximum(z, 0.0) + jnp.log1p(jnp.exp(-jnp.abs(z)))


def _softplus2(z2):
    sign_bit = jnp.uint32(0x80000000)
    neg_abs = lax.bitcast_convert_type(lax.bitcast_convert_type(z2, jnp.uint32) | sign_bit, F32)
    return jnp.maximum(z2, 0.0) + jnp.log2(1.0 + jnp.exp2(neg_abs))


def _load_row_tiles(ref, rows, first=0):
    return jnp.concatenate(
        [ref[pl.ds(first + c, rows, stride=SUBLANES), :] for c in range(SUBLANES)], axis=1)


def _store_row_tiles(ref, value):
    rows = value.shape[0]
    for c in range(SUBLANES):
        ref[pl.ds(c, rows, stride=SUBLANES), :] = value[:, c * LANES:(c + 1) * LANES]


def _rms(v, g):
    return v * lax.rsqrt(jnp.mean(v * v, axis=-1, keepdims=True) + RMS_EPS) * g


def _in_proj_kernel(x_ref, g_ref, wqk_ref, wvt_ref, wf_ref, bf_ref, sel_ref,
                    q_ref, kf_ref, ks_ref, aug_ref, vt_ref, carry_ref):
    @pl.when(pl.program_id(1) == 0)
    def _():
        carry_ref[...] = jnp.zeros_like(carry_ref)

    tm = x_ref.shape[1]
    mix = FOX_WIDTH + SB_WIDTH
    hb = _rms(x_ref[0], g_ref[...]).astype(BF16)
    qk = _dot(hb, wqk_ref[...])
    q_ref[0] = qk[:, :mix].astype(BF16)
    kf_ref[0] = qk[:, mix:mix + FOX_WIDTH].astype(BF16)
    ks_ref[0] = qk[:, mix + FOX_WIDTH:].astype(BF16)
    vt_ref[0] = _dot_nt(wvt_ref[...], hb).astype(BF16)
    log_f = -LOG2E * _softplus(-(_dot(hb, wf_ref[...]) + bf_ref[...]))
    row = lax.broadcasted_iota(jnp.int32, (tm, tm), 0)
    col = lax.broadcasted_iota(jnp.int32, (tm, tm), 1)
    tri = (col <= row).astype(BF16)
    hi, mid, lo = _split3(log_f)
    c = _dot(tri, hi) + _dot(tri, mid) + _dot(tri, lo) + carry_ref[...]
    carry_ref[...] = c[tm - 1:tm, :]
    c_hi, c_mid, c_lo = _split3(c)
    aug = _dot(c_hi, sel_ref[0]) + _dot(c_mid, sel_ref[1]) + _dot(c_lo, sel_ref[2])
    aug_ref[0] = aug.astype(BF16)


def _aug_selector():
    sel = np.zeros((3, LANES, FOX_WIDTH), np.float32)
    for t in range(3):
        for h in range(N_FOX_HEADS):
            sel[t, h, LANES * (h // 2) + 3 * (h % 2) + t] = -1.0
    return jnp.asarray(sel, BF16)


def _in_proj(x, g, wqk, wvt, wf, bf, sel):
    b, s, d = x.shape
    mix = FOX_WIDTH + SB_WIDTH
    tm = min(ROW_TILE, s)
    const2 = lambda i, j: (0, 0)
    return pl.pallas_call(
        _in_proj_kernel,
        name="in_proj",
        grid=(b, s // tm),
        in_specs=[
            pl.BlockSpec((1, tm, d), lambda i, j: (i, j, 0)),
            pl.BlockSpec((1, d), const2),
            pl.BlockSpec((d, 2 * mix), const2),
            pl.BlockSpec((mix, d), const2),
            pl.BlockSpec((d, LANES), const2),
            pl.BlockSpec((1, LANES), const2),
            pl.BlockSpec((3, LANES, FOX_WIDTH), lambda i, j: (0, 0, 0)),
        ],
        out_specs=[
            pl.BlockSpec((1, tm, mix), lambda i, j: (i, j, 0)),
            pl.BlockSpec((1, tm, FOX_WIDTH), lambda i, j: (i, j, 0)),
            pl.BlockSpec((1, tm, SB_WIDTH), lambda i, j: (i, j, 0)),
            pl.BlockSpec((1, tm, FOX_WIDTH), lambda i, j: (i, j, 0)),
            pl.BlockSpec((1, mix, tm), lambda i, j: (i, 0, j)),
        ],
        out_shape=[
            jax.ShapeDtypeStruct((b, s, mix), BF16),
            jax.ShapeDtypeStruct((b, s, FOX_WIDTH), BF16),
            jax.ShapeDtypeStruct((b, s, SB_WIDTH), BF16),
            jax.ShapeDtypeStruct((b, s, FOX_WIDTH), BF16),
            jax.ShapeDtypeStruct((b, mix, s), BF16),
        ],
        scratch_shapes=[pltpu.VMEM((1, LANES), F32)],
        compiler_params=pltpu.CompilerParams(
            dimension_semantics=("parallel", "arbitrary"),
            vmem_limit_bytes=VMEM_LIMIT),
    )(x, g, wqk, wvt, wf, bf, sel)


def _stack_heads(q, ones_cols):
    tq = q.shape[0]
    lane = lax.broadcasted_iota(jnp.int32, (tq, LANES), 1)
    low = lane < HEAD_DIM
    qf = q.astype(F32)
    halves = [jnp.where(low, qf, 0.0), jnp.where(low, 0.0, qf)]
    if ones_cols:
        halves = [jnp.concatenate([halves[hh], jnp.where(
            (lane >= 3 * hh) & (lane < 3 * hh + 3), 1.0, 0.0)], axis=1) for hh in range(2)]
    return jnp.concatenate(halves, axis=0).astype(q.dtype)


def _unstack_heads(acc_t, tq):
    sub = lax.broadcasted_iota(jnp.int32, (LANES, tq), 0)
    return jnp.where(sub < HEAD_DIM, acc_t[:, :tq], acc_t[:, tq:]).T


def _fox_kernel(q_ref, k_ref, aug_ref, vt_ref, o_ref,
                q2_ref, m_ref, l_ref, acc_ref, s_ref, p_ref, a_ref):
    i = pl.program_id(2)
    tq = q_ref.shape[1]
    q2_ref[...] = _stack_heads(q_ref[0], True)
    m_ref[...] = jnp.full(m_ref.shape, -1e30, F32)
    l_ref[...] = jnp.zeros_like(l_ref)
    acc_ref[...] = jnp.zeros_like(acc_ref)

    def scores(j, buf):
        k0 = pl.multiple_of(j * tq, tq)
        ka = jnp.concatenate(
            [k_ref[0, pl.ds(k0, tq), :], aug_ref[0, pl.ds(k0, tq), :]], axis=1)
        s_ref[buf] = _dot_nt(ka, q2_ref[...])

    def softmax_update(buf, diagonal):
        if diagonal:
            key = lax.broadcasted_iota(jnp.int32, (tq, 2 * tq), 0)
            qry = lax.broadcasted_iota(jnp.int32, (tq, 2 * tq), 1)
            qry = jnp.where(qry >= tq, qry - tq, qry)
            s_ref[buf] = jnp.where(key <= qry, s_ref[buf], -jnp.inf)
        m_old = m_ref[...]
        m_new = jnp.maximum(m_old, jnp.max(s_ref[buf], axis=0, keepdims=True))
        pr = jnp.exp2(s_ref[buf] - m_new)
        alpha = jnp.exp2(m_old - m_new)
        m_ref[...] = m_new
        l_ref[...] = l_ref[...] * alpha + jnp.sum(pr, axis=0, keepdims=True)
        p_ref[buf] = pr.astype(BF16)
        a_ref[buf] = alpha

    def accumulate(j, buf):
        k0 = pl.multiple_of(j * tq, tq)
        acc_ref[...] = (acc_ref[...] * a_ref[buf]
                        + _dot(vt_ref[0, :, pl.ds(k0, tq)], p_ref[buf]))

    def step(j, buf):
        scores(j + 1, 1 - buf)
        softmax_update(buf, False)
        accumulate(jnp.maximum(j - 1, 0), 1 - buf)

    def last(buf):
        softmax_update(buf, True)
        accumulate(jnp.maximum(i - 1, 0), 1 - buf)
        accumulate(i, buf)

    scores(0, 0)
    p_ref[1] = jnp.zeros(p_ref.shape[1:], BF16)
    a_ref[1] = jnp.ones(a_ref.shape[1:], F32)

    def two_steps(t, _):
        step(2 * t, 0)
        step(2 * t + 1, 1)
        return 0

    lax.fori_loop(0, i // 2, two_steps, 0)

    @pl.when(i % 2 == 1)
    def _():
        step(i - 1, 0)
        last(1)

    @pl.when(i % 2 == 0)
    def _():
        last(0)

    o_ref[0] = _unstack_heads(acc_ref[...] / l_ref[...], tq)


def _fox(q, kf, aug, vt):
    b, s, _ = q.shape
    tq = min(Q_TILE, s)
    return pl.pallas_call(
        _fox_kernel,
        name="fox",
        grid=(b, PAIRS, s // tq),
        in_specs=[
            pl.BlockSpec((1, tq, LANES), lambda bi, p, i: (bi, i, p)),
            pl.BlockSpec((1, s, LANES), lambda bi, p, i: (bi, 0, p)),
            pl.BlockSpec((1, s, LANES), lambda bi, p, i: (bi, 0, p)),
            pl.BlockSpec((1, LANES, s), lambda bi, p, i: (bi, p, 0)),
        ],
        out_specs=pl.BlockSpec((1, tq, LANES), lambda bi, p, i: (bi, i, p)),
        out_shape=jax.ShapeDtypeStruct((b, s, FOX_WIDTH), F32),
        scratch_shapes=[
            pltpu.VMEM((2 * tq, 2 * LANES), BF16),
            pltpu.VMEM((1, 2 * tq), F32),
            pltpu.VMEM((1, 2 * tq), F32),
            pltpu.VMEM((LANES, 2 * tq), F32),
            pltpu.VMEM((2, tq, 2 * tq), F32),
            pltpu.VMEM((2, tq, 2 * tq), BF16),
            pltpu.VMEM((2, 1, 2 * tq), F32),
        ],
        compiler_params=pltpu.CompilerParams(
            dimension_semantics=("parallel", "parallel", "parallel"),
            vmem_limit_bytes=VMEM_LIMIT),
    )(q, kf, aug, vt)


def _stick_kernel(q_ref, k_ref, vt_ref, o_ref, q2_ref, tail_ref, acc_ref, z_ref, a_ref):
    i = pl.program_id(2)
    tq = q_ref.shape[1]
    tc = min(CUM_CHUNK, tq)
    q2_ref[...] = _stack_heads(q_ref[0], False)
    tail_ref[...] = jnp.zeros_like(tail_ref)
    acc_ref[...] = jnp.zeros_like(acc_ref)
    kr = lax.broadcasted_iota(jnp.int32, (tc, tc), 0)
    kc = lax.broadcasted_iota(jnp.int32, (tc, tc), 1)
    later = (kc > kr).astype(BF16)

    def scores(j, buf):
        k0 = pl.multiple_of(j * tq, tq)
        z_ref[buf] = _dot_nt(k_ref[0, pl.ds(k0, tq), :], q2_ref[...])

    def weights(buf, diagonal):
        tail = tail_ref[...]
        for ch in reversed(range(tq // tc)):
            z_c = z_ref[buf, ch * tc:(ch + 1) * tc, :]
            sp_c = _softplus2(z_c)
            if diagonal:
                key = lax.broadcasted_iota(jnp.int32, (tc, 2 * tq), 0) + ch * tc
                qry = lax.broadcasted_iota(jnp.int32, (tc, 2 * tq), 1)
                qry = jnp.where(qry >= tq, qry - tq, qry)
                strict = key < qry
                sp_m = jnp.where(strict, sp_c, 0.0)
            else:
                sp_m = sp_c
            suffix = _dot(later, sp_m.astype(BF16))
            a = jnp.exp2((z_c - sp_c) - (suffix + tail))
            if diagonal:
                a = jnp.where(strict, a, 0.0)
            a_ref[buf, ch * tc:(ch + 1) * tc, :] = a.astype(BF16)
            tail = tail + suffix[0:1, :] + sp_m[0:1, :]
        tail_ref[...] = tail

    def accumulate(j, buf):
        k0 = pl.multiple_of(j * tq, tq)
        acc_ref[...] = acc_ref[...] + _dot(vt_ref[0, :, pl.ds(k0, tq)], a_ref[buf])

    def step(n, buf, diagonal):
        j = i - n
        scores(j - 1, 1 - buf)
        weights(buf, diagonal)
        if not diagonal:
            accumulate(j + 1, 1 - buf)

    def last(buf, diagonal):
        weights(buf, diagonal)
        if not diagonal:
            accumulate(1, 1 - buf)
        accumulate(0, buf)

    scores(i, 0)

    @pl.when(i == 0)
    def _():
        last(0, True)

    @pl.when(i > 0)
    def _():
        step(0, 0, True)

        def two_steps(t, _):
            step(1 + 2 * t, 1, False)
            step(2 + 2 * t, 0, False)
            return 0

        lax.fori_loop(0, (i - 1) // 2, two_steps, 0)

        @pl.when(i % 2 == 0)
        def _():
            step(i - 1, 1, False)
            last(0, False)

        @pl.when(i % 2 == 1)
        def _():
            last(1, False)

    o_ref[0] = _unstack_heads(acc_ref[...], tq)


def _stick(q, ks, vt):
    b, s, _ = q.shape
    tq = min(Q_TILE, s)
    return pl.pallas_call(
        _stick_kernel,
        name="stick",
        grid=(b, PAIRS, s // tq),
        in_specs=[
            pl.BlockSpec((1, tq, LANES), lambda bi, p, i: (bi, i, PAIRS + p)),
            pl.BlockSpec((1, s, LANES), lambda bi, p, i: (bi, 0, p)),
            pl.BlockSpec((1, LANES, s), lambda bi, p, i: (bi, PAIRS + p, 0)),
        ],
        out_specs=pl.BlockSpec((1, tq, LANES), lambda bi, p, i: (bi, i, p)),
        out_shape=jax.ShapeDtypeStruct((b, s, SB_WIDTH), F32),
        scratch_shapes=[
            pltpu.VMEM((2 * tq, LANES), BF16),
            pltpu.VMEM((1, 2 * tq), F32),
            pltpu.VMEM((LANES, 2 * tq), F32),
            pltpu.VMEM((2, tq, 2 * tq), F32),
            pltpu.VMEM((2, tq, 2 * tq), BF16),
        ],
        compiler_params=pltpu.CompilerParams(
            dimension_semantics=("parallel", "parallel", "parallel"),
            vmem_limit_bytes=VMEM_LIMIT),
    )(q, ks, vt)


def _out_proj_kernel(x_ref, fox_ref, sb_ref, gfox_ref, gsb_ref, wo_ref, g2_ref,
                     wr_ref, br_ref,
                     x1_ref, h2_ref, e_ref, w_ref, rank_ref, cnt_ref, carry_ref):
    @pl.when(pl.program_id(0) == 0)
    def _():
        carry_ref[...] = jnp.zeros_like(carry_ref)

    tm = x_ref.shape[0]
    a = _rms(fox_ref[...], gfox_ref[...]).astype(BF16)
    bb = _rms(sb_ref[...], gsb_ref[...]).astype(BF16)
    x1 = x_ref[...] + _dot(a, wo_ref[0:FOX_WIDTH, :]) + _dot(bb, wo_ref[FOX_WIDTH:, :])
    x1_ref[...] = x1
    h2 = _rms(x1, g2_ref[...])
    _store_row_tiles(h2_ref, h2)
    h_hi, h_lo = _split2(h2)
    w_hi, w_lo = _split2(wr_ref[...])
    logits = (_dot_nt(w_hi, h_hi) + _dot_nt(w_hi, h_lo) + _dot_nt(w_lo, h_hi)
              + br_ref[...])
    e_iota = lax.broadcasted_iota(jnp.int32, (N_EXPERTS, tm), 0)
    work = logits
    vals, idxs, sels = [], [], []
    for _ in range(TOP_K):
        m = jnp.max(work, axis=0, keepdims=True)
        idx = jnp.min(jnp.where(work == m, e_iota, N_EXPERTS), axis=0, keepdims=True)
        sel = e_iota == idx
        work = jnp.where(sel, -jnp.inf, work)
        vals.append(m)
        idxs.append(idx)
        sels.append(sel)
    exps = [jnp.exp(v - vals[0]) for v in vals]
    denom = exps[0] + exps[1] + exps[2] + exps[3]
    e_ref[...] = jnp.concatenate(idxs, axis=0)
    w_ref[...] = jnp.concatenate([ex / denom for ex in exps], axis=0)
    chosen = jnp.where(sels[0] | sels[1] | sels[2] | sels[3], 1.0, 0.0)
    row = lax.broadcasted_iota(jnp.int32, (tm, tm), 0)
    col = lax.broadcasted_iota(jnp.int32, (tm, tm), 1)
    before = (row < col).astype(BF16)
    rank = _dot(chosen.astype(BF16), before) + carry_ref[...]
    rank_ref[...] = jnp.concatenate(
        [jnp.sum(jnp.where(sel, rank, 0.0), axis=0, keepdims=True) for sel in sels],
        axis=0).astype(jnp.int32)
    total = carry_ref[...] + jnp.sum(chosen, axis=1, keepdims=True)
    carry_ref[...] = total
    cnt_ref[...] = jnp.broadcast_to(total, cnt_ref.shape)


def _out_proj(x2, fox2, sb2, gfox, gsb, wo, g2, wr_t, br):
    n, d = x2.shape
    tm = min(OUT_TILE, n)
    const = lambda i: (0, 0)
    return pl.pallas_call(
        _out_proj_kernel,
        name="out_proj",
        grid=(n // tm,),
        in_specs=[
            pl.BlockSpec((tm, d), lambda i: (i, 0)),
            pl.BlockSpec((tm, FOX_WIDTH), lambda i: (i, 0)),
            pl.BlockSpec((tm, SB_WIDTH), lambda i: (i, 0)),
            pl.BlockSpec((1, FOX_WIDTH), const),
            pl.BlockSpec((1, SB_WIDTH), const),
            pl.BlockSpec((FOX_WIDTH + SB_WIDTH, d), const),
            pl.BlockSpec((1, d), const),
            pl.BlockSpec((N_EXPERTS, d), const),
            pl.BlockSpec((N_EXPERTS, 1), const),
        ],
        out_specs=[
            pl.BlockSpec((tm, d), lambda i: (i, 0)),
            pl.BlockSpec((tm * SUBLANES, LANES), lambda i: (i, 0)),
            pl.BlockSpec((TOP_K, tm), lambda i: (0, i)),
            pl.BlockSpec((TOP_K, tm), lambda i: (0, i)),
            pl.BlockSpec((TOP_K, tm), lambda i: (0, i)),
            pl.BlockSpec((N_EXPERTS, LANES), const),
        ],
        out_shape=[
            jax.ShapeDtypeStruct((n, d), F32),
            jax.ShapeDtypeStruct((n * SUBLANES, LANES), F32),
            jax.ShapeDtypeStruct((TOP_K, n), jnp.int32),
            jax.ShapeDtypeStruct((TOP_K, n), F32),
            jax.ShapeDtypeStruct((TOP_K, n), jnp.int32),
            jax.ShapeDtypeStruct((N_EXPERTS, LANES), F32),
        ],
        scratch_shapes=[pltpu.VMEM((N_EXPERTS, 1), F32)],
        compiler_params=pltpu.CompilerParams(
            dimension_semantics=("arbitrary",),
            vmem_limit_bytes=VMEM_LIMIT),
    )(x2, fox2, sb2, gfox, gsb, wo, g2, wr_t, br)


def _route_kernel(cnt_ref, e_ref, rank_ref, dest_ref, start_ref):
    def per_expert(e, pstart):
        start_ref[e] = pstart
        return pstart + (cnt_ref[e] + EXPERT_TILE - 1) // EXPERT_TILE * EXPERT_TILE

    lax.fori_loop(0, N_EXPERTS, per_expert, 0)
    ev = e_ref[...]
    offs = jnp.zeros(ev.shape, jnp.int32)
    for e in range(N_EXPERTS):
        offs = jnp.where(ev == e, start_ref[e], offs)
    dest_ref[...] = rank_ref[...] + offs


def _route(counts, e_k, rank_k):
    smem = pl.BlockSpec(memory_space=pltpu.SMEM)
    vmem = pl.BlockSpec(memory_space=pltpu.VMEM)
    return pl.pallas_call(
        _route_kernel,
        name="route",
        in_specs=[smem, vmem, vmem],
        out_specs=[vmem, smem],
        out_shape=[
            jax.ShapeDtypeStruct(e_k.shape, jnp.int32),
            jax.ShapeDtypeStruct((N_EXPERTS,), jnp.int32),
        ],
    )(counts, e_k, rank_k)


def _zero_unused_blocks(dst_ref, zero_block_ref, sem, start_ref, cnt_ref):
    block_rows = zero_block_ref.shape[0]
    last = N_EXPERTS - 1
    used = (start_ref[last] + (cnt_ref[last] + EXPERT_TILE - 1) // EXPERT_TILE
            * EXPERT_TILE) * SUBLANES
    count = (dst_ref.shape[0] - used) // block_rows

    def copy(b):
        rows = pl.ds(pl.multiple_of(used + b * block_rows, block_rows), block_rows)
        return pltpu.make_async_copy(zero_block_ref, dst_ref.at[rows, :], sem)

    def start(b, _):
        copy(b).start()
        return 0

    def wait(b, _):
        copy(b).wait()
        return 0

    lax.fori_loop(0, count, start, 0)
    lax.fori_loop(0, count, wait, 0)


def _dispatch_kernel(dest_ref, start_ref, cnt_ref, h_ref, xb_ref, zero_ref, sem, pad_sem,
                     *, n_tok, tile):
    base = pl.program_id(0) * tile

    @pl.when(pl.program_id(0) == 0)
    def _():
        zero_ref[...] = jnp.zeros_like(zero_ref)

        def pad_rows(e, wait):
            n_pad = (-cnt_ref[e]) % EXPERT_TILE
            pos = start_ref[e] + cnt_ref[e]
            size = EXPERT_TILE // 2
            while size >= 1:
                @pl.when((n_pad & size) != 0)
                def _(pos=pos, size=size):
                    row = pl.multiple_of(pos * SUBLANES, SUBLANES)
                    copy = pltpu.make_async_copy(
                        zero_ref.at[pl.ds(0, size * SUBLANES), :],
                        xb_ref.at[pl.ds(row, size * SUBLANES), :], pad_sem)
                    if wait:
                        copy.wait()
                    else:
                        copy.start()

                pos = pos + (n_pad & size)
                size //= 2
            return 0

        lax.fori_loop(0, N_EXPERTS, lambda e, _: pad_rows(e, False), 0)
        lax.fori_loop(0, N_EXPERTS, lambda e, _: pad_rows(e, True), 0)
        _zero_unused_blocks(xb_ref, zero_ref, pad_sem, start_ref, cnt_ref)

    def issue(t, _):
        src = h_ref.at[pl.ds(pl.multiple_of(t * SUBLANES, SUBLANES), SUBLANES), :]
        for k in range(TOP_K):
            d = dest_ref[k * n_tok + base + t]
            pltpu.make_async_copy(
                src, xb_ref.at[pl.ds(pl.multiple_of(d * SUBLANES, SUBLANES), SUBLANES), :],
                sem).start(priority=k % 2)
        return 0

    lax.fori_loop(0, tile, issue, 0, unroll=8)
    rows = xb_ref.at[pl.ds(0, TOP_K * tile * SUBLANES), :]
    pltpu.make_async_copy(rows, rows, sem).wait()


def _dispatch(dest_flat, start, counts, h2, n_rows):
    n = h2.shape[0] // SUBLANES
    tile = min(DISPATCH_TILE, n)
    return pl.pallas_call(
        functools.partial(_dispatch_kernel, n_tok=n, tile=tile),
        name="dispatch",
        grid_spec=pltpu.PrefetchScalarGridSpec(
            num_scalar_prefetch=3,
            grid=(n // tile,),
            in_specs=[pl.BlockSpec((tile * SUBLANES, LANES), lambda i, dr, st, ct: (i, 0))],
            out_specs=pl.BlockSpec(memory_space=pl.ANY),
            scratch_shapes=[pltpu.VMEM((EXPERT_TILE * SUBLANES, LANES), F32),
                            pltpu.SemaphoreType.DMA(()),
                            pltpu.SemaphoreType.DMA(())],
        ),
        out_shape=jax.ShapeDtypeStruct((n_rows * SUBLANES, LANES), h2.dtype),
        compiler_params=pltpu.CompilerParams(dimension_semantics=("arbitrary",)),
    )(dest_flat, start, counts, h2)


def _experts_kernel(start_ref, cnt_ref, xb_ref, w1_ref, b1_ref, w2_ref, b2_ref, y_ref,
                    w1b_ref, w2b_ref, xbuf_ref, ybuf_ref, sem_in, sem_out):
    e = pl.program_id(0)
    d_ff = w2_ref.shape[1]
    block_rows = EXPERT_TILE * SUBLANES
    n_blk = (cnt_ref[e] + EXPERT_TILE - 1) // EXPERT_TILE
    first = start_ref[e] * SUBLANES
    w1b_ref[...] = w1_ref[0].astype(BF16)
    w2b_ref[...] = w2_ref[0].astype(BF16)

    def rows(j):
        return pl.ds(pl.multiple_of(first + j * block_rows, block_rows), block_rows)

    def copy_in(j, slot):
        return pltpu.make_async_copy(xb_ref.at[rows(j), :], xbuf_ref.at[slot], sem_in.at[slot])

    def copy_out(j, slot):
        return pltpu.make_async_copy(ybuf_ref.at[slot], y_ref.at[rows(j), :], sem_out.at[slot])

    @pl.when(n_blk > 0)
    def _():
        copy_in(0, 0).start()

    def block(j, _):
        slot = j % 2

        @pl.when(j + 1 < n_blk)
        def _():
            copy_in(j + 1, 1 - slot).start()

        copy_in(j, slot).wait()

        @pl.when(j >= 2)
        def _():
            copy_out(j - 2, slot).wait()

        x = _load_row_tiles(xbuf_ref.at[slot], EXPERT_TILE).astype(BF16)
        hh = _dot(x, w1b_ref[...]) + b1_ref[0]
        glu = jnp.minimum(hh[:, :d_ff], SWIGLU_LIMIT)
        lin = jnp.clip(hh[:, d_ff:], -SWIGLU_LIMIT, SWIGLU_LIMIT)
        act = glu * (1.0 / (1.0 + jnp.exp(-SWIGLU_ALPHA * glu))) * (lin + 1.0)
        _store_row_tiles(ybuf_ref.at[slot], _dot(act.astype(BF16), w2b_ref[...]) + b2_ref[0])
        copy_out(j, slot).start()
        return 0

    lax.fori_loop(0, n_blk, block, 0)

    for back in (1, 2):
        @pl.when(n_blk >= back)
        def _():
            copy_out(n_blk - back, (n_blk - back) % 2).wait()

    @pl.when(e == pl.num_programs(0) - 1)
    def _():
        ybuf_ref[0] = jnp.zeros(ybuf_ref.shape[1:], F32)
        _zero_unused_blocks(y_ref, ybuf_ref.at[0], sem_out.at[0], start_ref, cnt_ref)


def _experts(start, counts, xb, w1, b1, w2, b2):
    n_e, d, two_f = w1.shape
    d_ff = two_f // 2
    block_rows = EXPERT_TILE * SUBLANES
    any_spec = pl.BlockSpec(memory_space=pl.ANY)
    return pl.pallas_call(
        _experts_kernel,
        name="experts",
        grid_spec=pltpu.PrefetchScalarGridSpec(
            num_scalar_prefetch=2,
            grid=(n_e,),
            in_specs=[
                any_spec,
                pl.BlockSpec((1, d, two_f), lambda e, st, ct: (e, 0, 0)),
                pl.BlockSpec((1, 1, two_f), lambda e, st, ct: (e, 0, 0)),
                pl.BlockSpec((1, d_ff, d), lambda e, st, ct: (e, 0, 0)),
                pl.BlockSpec((1, 1, d), lambda e, st, ct: (e, 0, 0)),
            ],
            out_specs=any_spec,
            scratch_shapes=[
                pltpu.VMEM((d, two_f), BF16),
                pltpu.VMEM((d_ff, d), BF16),
                pltpu.VMEM((2, block_rows, LANES), F32),
                pltpu.VMEM((2, block_rows, LANES), F32),
                pltpu.SemaphoreType.DMA((2,)),
                pltpu.SemaphoreType.DMA((2,)),
            ],
        ),
        out_shape=jax.ShapeDtypeStruct(xb.shape, F32),
        compiler_params=pltpu.CompilerParams(
            dimension_semantics=("arbitrary",),
            vmem_limit_bytes=VMEM_LIMIT),
    )(start, counts, xb, w1, b1.reshape(n_e, 1, two_f), w2, b2.reshape(n_e, 1, d))


def _combine_kernel(dest_ref, x1_ref, w_ref, gf_ref, yb_ref, out_ref, buf_ref, sem,
                    *, n_tok):
    tile = x1_ref.shape[0]
    step = pl.program_id(0)
    slot = step % 2

    def gather(to_step, to_slot):
        def issue(t, _):
            for k in range(TOP_K):
                d = dest_ref[k * n_tok + to_step * tile + t]
                pltpu.make_async_copy(
                    yb_ref.at[pl.ds(pl.multiple_of(d * SUBLANES, SUBLANES), SUBLANES), :],
                    buf_ref.at[to_slot, pl.ds(pl.multiple_of((k * tile + t) * SUBLANES,
                                                              SUBLANES), SUBLANES), :],
                    sem.at[to_slot]).start(priority=k % 2)
            return 0

        lax.fori_loop(0, tile, issue, 0, unroll=8)

    @pl.when(step == 0)
    def _():
        gather(0, 0)

    @pl.when(step + 1 < pl.num_programs(0))
    def _():
        gather(step + 1, 1 - slot)

    pltpu.make_async_copy(yb_ref.at[pl.ds(0, TOP_K * tile * SUBLANES), :],
                          buf_ref.at[slot], sem.at[slot]).wait()
    acc = x1_ref[...]
    w = w_ref[...]
    rows = buf_ref.at[slot]
    for k in range(TOP_K):
        acc = acc + _load_row_tiles(rows, tile, k * tile * SUBLANES) * w[:, k:k + 1]
    out_ref[...] = _rms(acc, gf_ref[...])


def _combine(dest_flat, x1, w_tok, gf, yb):
    n, d = x1.shape
    tile = min(COMBINE_TILE, n)
    return pl.pallas_call(
        functools.partial(_combine_kernel, n_tok=n),
        name="combine",
        grid_spec=pltpu.PrefetchScalarGridSpec(
            num_scalar_prefetch=1,
            grid=(n // tile,),
            in_specs=[
                pl.BlockSpec((tile, d), lambda i, dr: (i, 0)),
                pl.BlockSpec((tile, TOP_K), lambda i, dr: (i, 0)),
                pl.BlockSpec((1, d), lambda i, dr: (0, 0)),
                pl.BlockSpec(memory_space=pl.ANY),
            ],
            out_specs=pl.BlockSpec((tile, d), lambda i, dr: (i, 0)),
            scratch_shapes=[pltpu.VMEM((2, TOP_K * tile * SUBLANES, LANES), F32),
                            pltpu.SemaphoreType.DMA((2,))],
        ),
        out_shape=jax.ShapeDtypeStruct((n, d), F32),
        compiler_params=pltpu.CompilerParams(
            dimension_semantics=("arbitrary",),
            vmem_limit_bytes=VMEM_LIMIT),
    )(dest_flat, x1, w_tok, gf, yb)


def kernel(x, norm1_g, w_in, b_f, g_fox, g_sb, w_out, norm2_g, w_router, b_router,
           w1, b1, w2, b2, norm_f_g):
    b, s, d = x.shape
    n = b * s
    depth = norm1_g.shape[0]
    scale = HEAD_DIM ** -0.5 * LOG2E
    fw, sw = FOX_WIDTH, SB_WIDTH
    n_blocks = -(-n * TOP_K // EXPERT_TILE) + N_EXPERTS
    n_rows = n_blocks * EXPERT_TILE
    assert depth == 1, "the combine kernel fuses the final norm into the only layer"
    assert d == SUBLANES * LANES, "row-tile layout holds one model row per (8, 128) tile"
    wl = w_in[0]
    f0 = 3 * fw
    s0 = f0 + N_FOX_HEADS
    w_qk = jnp.concatenate(
        [wl[:, :fw] * scale, wl[:, s0:s0 + sw] * scale,
         wl[:, fw:2 * fw], wl[:, s0 + sw:s0 + 2 * sw]], axis=1).astype(BF16)
    w_vt = jnp.concatenate([wl[:, 2 * fw:f0], wl[:, s0 + 2 * sw:]], axis=1).T.astype(BF16)
    w_f = jnp.pad(wl[:, f0:s0], ((0, 0), (0, LANES - N_FOX_HEADS))).astype(BF16)
    bias_f = jnp.pad(b_f[0], (0, LANES - N_FOX_HEADS))[None, :]
    q, kf, ks, aug, vt = _in_proj(x, norm1_g[0][None, :], w_qk, w_vt, w_f, bias_f,
                                  _aug_selector())
    fox = _fox(q, kf, aug, vt)
    sb = _stick(q, ks, vt)
    x1, h2, e_k, w_k, rank_k, counts = _out_proj(
        x.reshape(n, d), fox.reshape(n, fw), sb.reshape(n, sw),
        g_fox[0][None, :], g_sb[0][None, :], w_out[0].astype(BF16),
        norm2_g[0][None, :], w_router[0].T, b_router[0][:, None])
    counts = counts[:, 0].astype(jnp.int32)
    dest, start = _route(counts, e_k, rank_k)
    dest_flat = dest.reshape(-1)
    xb = _dispatch(dest_flat, start, counts, h2, n_rows)
    yb = _experts(start, counts, xb, w1[0], b1[0], w2[0], b2[0])
    out = _combine(dest_flat, x1, w_k.T, norm_f_g[None, :], yb)
    return out.reshape(b, s, d)
```

```python
import functools

import jax
import jax.numpy as jnp
import numpy as np
from jax import lax
from jax.experimental import pallas as pl
from jax.experimental.pallas import tpu as pltpu

HEAD_DIM = 64
N_FOX_HEADS = 8
N_SB_HEADS = 8
FOX_WIDTH = N_FOX_HEADS * HEAD_DIM
SB_WIDTH = N_SB_HEADS * HEAD_DIM
N_EXPERTS = 32
TOP_K = 4
SWIGLU_LIMIT = 7.0
SWIGLU_ALPHA = 1.702
RMS_EPS = 1e-5
LOG2E = 1.4426950408889634

LANES = 128
SUBLANES = 8
PAIRS = FOX_WIDTH // LANES

ROW_TILE = 256
OUT_TILE = 1024
Q_TILE = 512
CUM_CHUNK = 256
EXPERT_TILE = 256
DISPATCH_TILE = 1024
COMBINE_TILE = 512
VMEM_LIMIT = 56 * 1024 * 1024

F32 = jnp.float32
BF16 = jnp.bfloat16


def _dot(a, b):
    return jnp.dot(a, b, preferred_element_type=F32)


def _dot_nt(a, b):
    return lax.dot_general(a, b, (((1,), (1,)), ((), ())), preferred_element_type=F32)


def _split2(v):
    hi = v.astype(BF16)
    lo = (v - hi.astype(F32)).astype(BF16)
    return hi, lo


def _split3(v):
    hi = v.astype(BF16)
    r = v - hi.astype(F32)
    mid = r.astype(BF16)
    lo = (r - mid.astype(F32)).astype(BF16)
    return hi, mid, lo


def _softplus(z):
    return jnp.maximum(z, 0.0) + jnp.log1p(jnp.exp(-jnp.abs(z)))


def _softplus2(z2):
    sign_bit = jnp.uint32(0x80000000)
    neg_abs = lax.bitcast_convert_type(lax.bitcast_convert_type(z2, jnp.uint32) | sign_bit, F32)
    return jnp.maximum(z2, 0.0) + jnp.log2(1.0 + jnp.exp2(neg_abs))


def _load_row_tiles(ref, rows, first=0):
    return jnp.concatenate(
        [ref[pl.ds(first + c, rows, stride=SUBLANES), :] for c in range(SUBLANES)], axis=1)


def _store_row_tiles(ref, value):
    rows = value.shape[0]
    for c in range(SUBLANES):
        ref[pl.ds(c, rows, stride=SUBLANES), :] = value[:, c * LANES:(c + 1) * LANES]


def _rms(v, g):
    return v * lax.rsqrt(jnp.mean(v * v, axis=-1, keepdims=True) + RMS_EPS) * g


def _in_proj_kernel(x_ref, g_ref, wqk_ref, wvt_ref, wf_ref, bf_ref, sel_ref,
                    q_ref, kf_ref, ks_ref, aug_ref, vt_ref, carry_ref):
    @pl.when(pl.program_id(1) == 0)
    def _():
        carry_ref[...] = jnp.zeros_like(carry_ref)

    tm = x_ref.shape[1]
    mix = FOX_WIDTH + SB_WIDTH
    hb = _rms(x_ref[0], g_ref[...]).astype(BF16)
    qk = _dot(hb, wqk_ref[...])
    q_ref[0] = qk[:, :mix].astype(BF16)
    kf_ref[0] = qk[:, mix:mix + FOX_WIDTH].astype(BF16)
    ks_ref[0] = qk[:, mix + FOX_WIDTH:].astype(BF16)
    vt_ref[0] = _dot_nt(wvt_ref[...], hb).astype(BF16)
    log_f = -LOG2E * _softplus(-(_dot(hb, wf_ref[...]) + bf_ref[...]))
    row = lax.broadcasted_iota(jnp.int32, (tm, tm), 0)
    col = lax.broadcasted_iota(jnp.int32, (tm, tm), 1)
    tri = (col <= row).astype(BF16)
    hi, mid, lo = _split3(log_f)
    c = _dot(tri, hi) + _dot(tri, mid) + _dot(tri, lo) + carry_ref[...]
    carry_ref[...] = c[tm - 1:tm, :]
    c_hi, c_mid, c_lo = _split3(c)
    aug = _dot(c_hi, sel_ref[0]) + _dot(c_mid, sel_ref[1]) + _dot(c_lo, sel_ref[2])
    aug_ref[0] = aug.astype(BF16)


def _aug_selector():
    sel = np.zeros((3, LANES, FOX_WIDTH), np.float32)
    for t in range(3):
        for h in range(N_FOX_HEADS):
            sel[t, h, LANES * (h // 2) + 3 * (h % 2) + t] = -1.0
    return jnp.asarray(sel, BF16)


def _in_proj(x, g, wqk, wvt, wf, bf, sel):
    b, s, d = x.shape
    mix = FOX_WIDTH + SB_WIDTH
    tm = min(ROW_TILE, s)
    const2 = lambda i, j: (0, 0)
    return pl.pallas_call(
        _in_proj_kernel,
        name="in_proj",
        grid=(b, s // tm),
        in_specs=[
            pl.BlockSpec((1, tm, d), lambda i, j: (i, j, 0)),
            pl.BlockSpec((1, d), const2),
            pl.BlockSpec((d, 2 * mix), const2),
            pl.BlockSpec((mix, d), const2),
            pl.BlockSpec((d, LANES), const2),
            pl.BlockSpec((1, LANES), const2),
            pl.BlockSpec((3, LANES, FOX_WIDTH), lambda i, j: (0, 0, 0)),
        ],
        out_specs=[
            pl.BlockSpec((1, tm, mix), lambda i, j: (i, j, 0)),
            pl.BlockSpec((1, tm, FOX_WIDTH), lambda i, j: (i, j, 0)),
            pl.BlockSpec((1, tm, SB_WIDTH), lambda i, j: (i, j, 0)),
            pl.BlockSpec((1, tm, FOX_WIDTH), lambda i, j: (i, j, 0)),
            pl.BlockSpec((1, mix, tm), lambda i, j: (i, 0, j)),
        ],
        out_shape=[
            jax.ShapeDtypeStruct((b, s, mix), BF16),
            jax.ShapeDtypeStruct((b, s, FOX_WIDTH), BF16),
            jax.ShapeDtypeStruct((b, s, SB_WIDTH), BF16),
            jax.ShapeDtypeStruct((b, s, FOX_WIDTH), BF16),
            jax.ShapeDtypeStruct((b, mix, s), BF16),
        ],
        scratch_shapes=[pltpu.VMEM((1, LANES), F32)],
        compiler_params=pltpu.CompilerParams(
            dimension_semantics=("parallel", "arbitrary"),
            vmem_limit_bytes=VMEM_LIMIT),
    )(x, g, wqk, wvt, wf, bf, sel)


def _stack_heads(q, ones_cols):
    tq = q.shape[0]
    lane = lax.broadcasted_iota(jnp.int32, (tq, LANES), 1)
    low = lane < HEAD_DIM
    qf = q.astype(F32)
    halves = [jnp.where(low, qf, 0.0), jnp.where(low, 0.0, qf)]
    if ones_cols:
        halves = [jnp.concatenate([halves[hh], jnp.where(
            (lane >= 3 * hh) & (lane < 3 * hh + 3), 1.0, 0.0)], axis=1) for hh in range(2)]
    return jnp.concatenate(halves, axis=0).astype(q.dtype)


def _unstack_heads(acc_t, tq):
    sub = lax.broadcasted_iota(jnp.int32, (LANES, tq), 0)
    return jnp.where(sub < HEAD_DIM, acc_t[:, :tq], acc_t[:, tq:]).T


def _fox_kernel(*refs, tq):
    n_q = refs[0].shape[1] // tq
    lax.fori_loop(0, n_q, lambda i, _: _fox_query_block(i, tq, *refs), 0)


def _fox_query_block(i, tq, q_ref, k_ref, aug_ref, vt_ref, o_ref,
                     q2_ref, m_ref, l_ref, acc_ref, s_ref, p_ref, a_ref):
    q_rows = pl.ds(pl.multiple_of(i * tq, tq), tq)
    q2_ref[...] = _stack_heads(q_ref[0, q_rows, :], True)
    m_ref[...] = jnp.full(m_ref.shape, -1e30, F32)
    l_ref[...] = jnp.zeros_like(l_ref)
    acc_ref[...] = jnp.zeros_like(acc_ref)

    def scores(j, buf):
        k0 = pl.multiple_of(j * tq, tq)
        ka = jnp.concatenate(
            [k_ref[0, pl.ds(k0, tq), :], aug_ref[0, pl.ds(k0, tq), :]], axis=1)
        s_ref[buf] = _dot_nt(ka, q2_ref[...])

    def softmax_update(buf, diagonal):
        if diagonal:
            key = lax.broadcasted_iota(jnp.int32, (tq, 2 * tq), 0)
            qry = lax.broadcasted_iota(jnp.int32, (tq, 2 * tq), 1)
            qry = jnp.where(qry >= tq, qry - tq, qry)
            s_ref[buf] = jnp.where(key <= qry, s_ref[buf], -jnp.inf)
        m_old = m_ref[...]
        m_new = jnp.maximum(m_old, jnp.max(s_ref[buf], axis=0, keepdims=True))
        pr = jnp.exp2(s_ref[buf] - m_new)
        alpha = jnp.exp2(m_old - m_new)
        m_ref[...] = m_new
        l_ref[...] = l_ref[...] * alpha + jnp.sum(pr, axis=0, keepdims=True)
        p_ref[buf] = pr.astype(BF16)
        a_ref[buf] = alpha

    def accumulate(j, buf):
        k0 = pl.multiple_of(j * tq, tq)
        acc_ref[...] = (acc_ref[...] * a_ref[buf]
                        + _dot(vt_ref[0, :, pl.ds(k0, tq)], p_ref[buf]))

    def step(j, buf):
        scores(j + 1, 1 - buf)
        softmax_update(buf, False)
        accumulate(jnp.maximum(j - 1, 0), 1 - buf)

    def last(buf):
        softmax_update(buf, True)
        accumulate(jnp.maximum(i - 1, 0), 1 - buf)
        accumulate(i, buf)

    scores(0, 0)
    p_ref[1] = jnp.zeros(p_ref.shape[1:], BF16)
    a_ref[1] = jnp.ones(a_ref.shape[1:], F32)

    def two_steps(t, _):
        step(2 * t, 0)
        step(2 * t + 1, 1)
        return 0

    lax.fori_loop(0, i // 2, two_steps, 0)

    @pl.when(i % 2 == 1)
    def _():
        step(i - 1, 0)
        last(1)

    @pl.when(i % 2 == 0)
    def _():
        last(0)

    o_ref[0, q_rows, :] = _unstack_heads(acc_ref[...] / l_ref[...], tq)
    return 0


def _fox(q, kf, aug, vt):
    b, s, _ = q.shape
    tq = min(Q_TILE, s)
    return pl.pallas_call(
        functools.partial(_fox_kernel, tq=tq),
        name="fox",
        grid=(b, PAIRS),
        in_specs=[
            pl.BlockSpec((1, s, LANES), lambda bi, p: (bi, 0, p)),
            pl.BlockSpec((1, s, LANES), lambda bi, p: (bi, 0, p)),
            pl.BlockSpec((1, s, LANES), lambda bi, p: (bi, 0, p)),
            pl.BlockSpec((1, LANES, s), lambda bi, p: (bi, p, 0)),
        ],
        out_specs=pl.BlockSpec((1, s, LANES), lambda bi, p: (bi, 0, p)),
        out_shape=jax.ShapeDtypeStruct((b, s, FOX_WIDTH), F32),
        scratch_shapes=[
            pltpu.VMEM((2 * tq, 2 * LANES), BF16),
            pltpu.VMEM((1, 2 * tq), F32),
            pltpu.VMEM((1, 2 * tq), F32),
            pltpu.VMEM((LANES, 2 * tq), F32),
            pltpu.VMEM((2, tq, 2 * tq), F32),
            pltpu.VMEM((2, tq, 2 * tq), BF16),
            pltpu.VMEM((2, 1, 2 * tq), F32),
        ],
        compiler_params=pltpu.CompilerParams(
            dimension_semantics=("parallel", "parallel"),
            vmem_limit_bytes=VMEM_LIMIT),
    )(q, kf, aug, vt)


def _stick_kernel(*refs, tq):
    n_q = refs[0].shape[1] // tq
    lax.fori_loop(0, n_q, lambda i, _: _stick_query_block(i, tq, *refs), 0)


def _stick_query_block(i, tq, q_ref, k_ref, vt_ref, o_ref,
                       q2_ref, tail_ref, acc_ref, z_ref, a_ref):
    tc = min(CUM_CHUNK, tq)
    q_rows = pl.ds(pl.multiple_of(i * tq, tq), tq)
    q2_ref[...] = _stack_heads(q_ref[0, q_rows, :], False)
    tail_ref[...] = jnp.zeros_like(tail_ref)
    acc_ref[...] = jnp.zeros_like(acc_ref)
    kr = lax.broadcasted_iota(jnp.int32, (tc, tc), 0)
    kc = lax.broadcasted_iota(jnp.int32, (tc, tc), 1)
    later = (kc > kr).astype(BF16)

    def scores(j, buf):
        k0 = pl.multiple_of(j * tq, tq)
        z_ref[buf] = _dot_nt(k_ref[0, pl.ds(k0, tq), :], q2_ref[...])

    def weights(buf, diagonal):
        tail = tail_ref[...]
        for ch in reversed(range(tq // tc)):
            z_c = z_ref[buf, ch * tc:(ch + 1) * tc, :]
            sp_c = _softplus2(z_c)
            if diagonal:
                key = lax.broadcasted_iota(jnp.int32, (tc, 2 * tq), 0) + ch * tc
                qry = lax.broadcasted_iota(jnp.int32, (tc, 2 * tq), 1)
                qry = jnp.where(qry >= tq, qry - tq, qry)
                strict = key < qry
                sp_m = jnp.where(strict, sp_c, 0.0)
            else:
                sp_m = sp_c
            suffix = _dot(later, sp_m.astype(BF16))
            a = jnp.exp2((z_c - sp_c) - (suffix + tail))
            if diagonal:
                a = jnp.where(strict, a, 0.0)
            a_ref[buf, ch * tc:(ch + 1) * tc, :] = a.astype(BF16)
            tail = tail + suffix[0:1, :] + sp_m[0:1, :]
        tail_ref[...] = tail

    def accumulate(j, buf):
        k0 = pl.multiple_of(j * tq, tq)
        acc_ref[...] = acc_ref[...] + _dot(vt_ref[0, :, pl.ds(k0, tq)], a_ref[buf])

    def step(n, buf, diagonal):
        j = i - n
        scores(j - 1, 1 - buf)
        weights(buf, diagonal)
        if not diagonal:
            accumulate(j + 1, 1 - buf)

    def last(buf, diagonal):
        weights(buf, diagonal)
        if not diagonal:
            accumulate(1, 1 - buf)
        accumulate(0, buf)

    scores(i, 0)

    @pl.when(i == 0)
    def _():
        last(0, True)

    @pl.when(i > 0)
    def _():
        step(0, 0, True)

        def two_steps(t, _):
            step(1 + 2 * t, 1, False)
            step(2 + 2 * t, 0, False)
            return 0

        lax.fori_loop(0, (i - 1) // 2, two_steps, 0)

        @pl.when(i % 2 == 0)
        def _():
            step(i - 1, 1, False)
            last(0, False)

        @pl.when(i % 2 == 1)
        def _():
            last(1, False)

    o_ref[0, q_rows, :] = _unstack_heads(acc_ref[...], tq)
    return 0


def _stick(q, ks, vt):
    b, s, _ = q.shape
    tq = min(Q_TILE, s)
    return pl.pallas_call(
        functools.partial(_stick_kernel, tq=tq),
        name="stick",
        grid=(b, PAIRS),
        in_specs=[
            pl.BlockSpec((1, s, LANES), lambda bi, p: (bi, 0, PAIRS + p)),
            pl.BlockSpec((1, s, LANES), lambda bi, p: (bi, 0, p)),
            pl.BlockSpec((1, LANES, s), lambda bi, p: (bi, PAIRS + p, 0)),
        ],
        out_specs=pl.BlockSpec((1, s, LANES), lambda bi, p: (bi, 0, p)),
        out_shape=jax.ShapeDtypeStruct((b, s, SB_WIDTH), F32),
        scratch_shapes=[
            pltpu.VMEM((2 * tq, LANES), BF16),
            pltpu.VMEM((1, 2 * tq), F32),
            pltpu.VMEM((LANES, 2 * tq), F32),
            pltpu.VMEM((2, tq, 2 * tq), F32),
            pltpu.VMEM((2, tq, 2 * tq), BF16),
        ],
        compiler_params=pltpu.CompilerParams(
            dimension_semantics=("parallel", "parallel"),
            vmem_limit_bytes=VMEM_LIMIT),
    )(q, ks, vt)


def _out_proj_kernel(x_ref, fox_ref, sb_ref, gfox_ref, gsb_ref, wo_ref, g2_ref,
                     wr_ref, br_ref,
                     x1_ref, h2_ref, e_ref, w_ref, rank_ref, cnt_ref, carry_ref):
    @pl.when(pl.program_id(0) == 0)
    def _():
        carry_ref[...] = jnp.zeros_like(carry_ref)

    tm = x_ref.shape[0]
    a = _rms(fox_ref[...], gfox_ref[...]).astype(BF16)
    bb = _rms(sb_ref[...], gsb_ref[...]).astype(BF16)
    x1 = x_ref[...] + _dot(a, wo_ref[0:FOX_WIDTH, :]) + _dot(bb, wo_ref[FOX_WIDTH:, :])
    x1_ref[...] = x1
    h2 = _rms(x1, g2_ref[...])
    _store_row_tiles(h2_ref, h2)
    h_hi, h_lo = _split2(h2)
    w_hi, w_lo = _split2(wr_ref[...])
    logits = (_dot_nt(w_hi, h_hi) + _dot_nt(w_hi, h_lo) + _dot_nt(w_lo, h_hi)
              + br_ref[...])
    e_iota = lax.broadcasted_iota(jnp.int32, (N_EXPERTS, tm), 0)
    work = logits
    vals, idxs, sels = [], [], []
    for _ in range(TOP_K):
        m = jnp.max(work, axis=0, keepdims=True)
        idx = jnp.min(jnp.where(work == m, e_iota, N_EXPERTS), axis=0, keepdims=True)
        sel = e_iota == idx
        work = jnp.where(sel, -jnp.inf, work)
        vals.append(m)
        idxs.append(idx)
        sels.append(sel)
    exps = [jnp.exp(v - vals[0]) for v in vals]
    denom = exps[0] + exps[1] + exps[2] + exps[3]
    e_ref[...] = jnp.concatenate(idxs, axis=0)
    w_ref[...] = jnp.concatenate([ex / denom for ex in exps], axis=0)
    chosen = jnp.where(sels[0] | sels[1] | sels[2] | sels[3], 1.0, 0.0)
    row = lax.broadcasted_iota(jnp.int32, (tm, tm), 0)
    col = lax.broadcasted_iota(jnp.int32, (tm, tm), 1)
    before = (row < col).astype(BF16)
    rank = _dot(chosen.astype(BF16), before) + carry_ref[...]
    rank_ref[...] = jnp.concatenate(
        [jnp.sum(jnp.where(sel, rank, 0.0), axis=0, keepdims=True) for sel in sels],
        axis=0).astype(jnp.int32)
    total = carry_ref[...] + jnp.sum(chosen, axis=1, keepdims=True)
    carry_ref[...] = total
    cnt_ref[...] = jnp.broadcast_to(total, cnt_ref.shape)


def _out_proj(x2, fox2, sb2, gfox, gsb, wo, g2, wr_t, br):
    n, d = x2.shape
    tm = min(OUT_TILE, n)
    const = lambda i: (0, 0)
    return pl.pallas_call(
        _out_proj_kernel,
        name="out_proj",
        grid=(n // tm,),
        in_specs=[
            pl.BlockSpec((tm, d), lambda i: (i, 0)),
            pl.BlockSpec((tm, FOX_WIDTH), lambda i: (i, 0)),
            pl.BlockSpec((tm, SB_WIDTH), lambda i: (i, 0)),
            pl.BlockSpec((1, FOX_WIDTH), const),
            pl.BlockSpec((1, SB_WIDTH), const),
            pl.BlockSpec((FOX_WIDTH + SB_WIDTH, d), const),
            pl.BlockSpec((1, d), const),
            pl.BlockSpec((N_EXPERTS, d), const),
            pl.BlockSpec((N_EXPERTS, 1), const),
        ],
        out_specs=[
            pl.BlockSpec((tm, d), lambda i: (i, 0)),
            pl.BlockSpec((tm * SUBLANES, LANES), lambda i: (i, 0)),
            pl.BlockSpec((TOP_K, tm), lambda i: (0, i)),
            pl.BlockSpec((TOP_K, tm), lambda i: (0, i)),
            pl.BlockSpec((TOP_K, tm), lambda i: (0, i)),
            pl.BlockSpec((N_EXPERTS, LANES), const),
        ],
        out_shape=[
            jax.ShapeDtypeStruct((n, d), F32),
            jax.ShapeDtypeStruct((n * SUBLANES, LANES), F32),
            jax.ShapeDtypeStruct((TOP_K, n), jnp.int32),
            jax.ShapeDtypeStruct((TOP_K, n), F32),
            jax.ShapeDtypeStruct((TOP_K, n), jnp.int32),
            jax.ShapeDtypeStruct((N_EXPERTS, LANES), F32),
        ],
        scratch_shapes=[pltpu.VMEM((N_EXPERTS, 1), F32)],
        compiler_params=pltpu.CompilerParams(
            dimension_semantics=("arbitrary",),
            vmem_limit_bytes=VMEM_LIMIT),
    )(x2, fox2, sb2, gfox, gsb, wo, g2, wr_t, br)


def _route_kernel(cnt_ref, e_ref, rank_ref, dest_ref, start_ref):
    def per_expert(e, pstart):
        start_ref[e] = pstart
        return pstart + (cnt_ref[e] + EXPERT_TILE - 1) // EXPERT_TILE * EXPERT_TILE

    lax.fori_loop(0, N_EXPERTS, per_expert, 0)
    ev = e_ref[...]
    offs = jnp.zeros(ev.shape, jnp.int32)
    for e in range(N_EXPERTS):
        offs = jnp.where(ev == e, start_ref[e], offs)
    dest_ref[...] = rank_ref[...] + offs


def _route(counts, e_k, rank_k):
    smem = pl.BlockSpec(memory_space=pltpu.SMEM)
    vmem = pl.BlockSpec(memory_space=pltpu.VMEM)
    return pl.pallas_call(
        _route_kernel,
        name="route",
        in_specs=[smem, vmem, vmem],
        out_specs=[vmem, smem],
        out_shape=[
            jax.ShapeDtypeStruct(e_k.shape, jnp.int32),
            jax.ShapeDtypeStruct((N_EXPERTS,), jnp.int32),
        ],
    )(counts, e_k, rank_k)


def _zero_unused_blocks(dst_ref, zero_block_ref, sem, start_ref, cnt_ref):
    block_rows = zero_block_ref.shape[0]
    last = N_EXPERTS - 1
    used = (start_ref[last] + (cnt_ref[last] + EXPERT_TILE - 1) // EXPERT_TILE
            * EXPERT_TILE) * SUBLANES
    count = (dst_ref.shape[0] - used) // block_rows

    def copy(b):
        rows = pl.ds(pl.multiple_of(used + b * block_rows, block_rows), block_rows)
        return pltpu.make_async_copy(zero_block_ref, dst_ref.at[rows, :], sem)

    def start(b, _):
        copy(b).start()
        return 0

    def wait(b, _):
        copy(b).wait()
        return 0

    lax.fori_loop(0, count, start, 0)
    lax.fori_loop(0, count, wait, 0)


def _dispatch_kernel(dest_ref, start_ref, cnt_ref, h_ref, xb_ref, zero_ref, sem, pad_sem,
                     *, n_tok, tile):
    base = pl.program_id(0) * tile

    @pl.when(pl.program_id(0) == 0)
    def _():
        zero_ref[...] = jnp.zeros_like(zero_ref)

        def pad_rows(e, wait):
            n_pad = (-cnt_ref[e]) % EXPERT_TILE
            pos = start_ref[e] + cnt_ref[e]
            size = EXPERT_TILE // 2
            while size >= 1:
                @pl.when((n_pad & size) != 0)
                def _(pos=pos, size=size):
                    row = pl.multiple_of(pos * SUBLANES, SUBLANES)
                    copy = pltpu.make_async_copy(
                        zero_ref.at[pl.ds(0, size * SUBLANES), :],
                        xb_ref.at[pl.ds(row, size * SUBLANES), :], pad_sem)
                    if wait:
                        copy.wait()
                    else:
                        copy.start()

                pos = pos + (n_pad & size)
                size //= 2
            return 0

        lax.fori_loop(0, N_EXPERTS, lambda e, _: pad_rows(e, False), 0)
        lax.fori_loop(0, N_EXPERTS, lambda e, _: pad_rows(e, True), 0)
        _zero_unused_blocks(xb_ref, zero_ref, pad_sem, start_ref, cnt_ref)

    def issue(t, _):
        src = h_ref.at[pl.ds(pl.multiple_of(t * SUBLANES, SUBLANES), SUBLANES), :]
        for k in range(TOP_K):
            d = dest_ref[k * n_tok + base + t]
            pltpu.make_async_copy(
                src, xb_ref.at[pl.ds(pl.multiple_of(d * SUBLANES, SUBLANES), SUBLANES), :],
                sem).start(priority=k % 2)
        return 0

    lax.fori_loop(0, tile, issue, 0, unroll=8)
    rows = xb_ref.at[pl.ds(0, TOP_K * tile * SUBLANES), :]
    pltpu.make_async_copy(rows, rows, sem).wait()


def _dispatch(dest_flat, start, counts, h2, n_rows):
    n = h2.shape[0] // SUBLANES
    tile = min(DISPATCH_TILE, n)
    return pl.pallas_call(
        functools.partial(_dispatch_kernel, n_tok=n, tile=tile),
        name="dispatch",
        grid_spec=pltpu.PrefetchScalarGridSpec(
            num_scalar_prefetch=3,
            grid=(n // tile,),
            in_specs=[pl.BlockSpec((tile * SUBLANES, LANES), lambda i, dr, st, ct: (i, 0))],
            out_specs=pl.BlockSpec(memory_space=pl.ANY),
            scratch_shapes=[pltpu.VMEM((EXPERT_TILE * SUBLANES, LANES), F32),
                            pltpu.SemaphoreType.DMA(()),
                            pltpu.SemaphoreType.DMA(())],
        ),
        out_shape=jax.ShapeDtypeStruct((n_rows * SUBLANES, LANES), h2.dtype),
        compiler_params=pltpu.CompilerParams(dimension_semantics=("arbitrary",)),
    )(dest_flat, start, counts, h2)


def _experts_kernel(start_ref, cnt_ref, xb_ref, w1_ref, b1_ref, w2_ref, b2_ref, y_ref,
                    w1b_ref, w2b_ref, xbuf_ref, ybuf_ref, sem_in, sem_out):
    e = pl.program_id(0)
    d_ff = w2_ref.shape[1]
    block_rows = EXPERT_TILE * SUBLANES
    n_blk = (cnt_ref[e] + EXPERT_TILE - 1) // EXPERT_TILE
    first = start_ref[e] * SUBLANES

    def rows(j):
        return pl.ds(pl.multiple_of(first + j * block_rows, block_rows), block_rows)

    def copy_in(j, slot):
        return pltpu.make_async_copy(xb_ref.at[rows(j), :], xbuf_ref.at[slot], sem_in.at[slot])

    def copy_out(j, slot):
        return pltpu.make_async_copy(ybuf_ref.at[slot], y_ref.at[rows(j), :], sem_out.at[slot])

    @pl.when(n_blk > 0)
    def _():
        copy_in(0, 0).start(priority=1)

    w1b_ref[...] = w1_ref[0].astype(BF16)
    w2b_ref[...] = w2_ref[0].astype(BF16)

    def block(j, _):
        slot = j % 2

        @pl.when(j + 1 < n_blk)
        def _():
            copy_in(j + 1, 1 - slot).start(priority=1)

        copy_in(j, slot).wait()

        @pl.when(j >= 2)
        def _():
            copy_out(j - 2, slot).wait()

        x = _load_row_tiles(xbuf_ref.at[slot], EXPERT_TILE).astype(BF16)
        hh = _dot(x, w1b_ref[...]) + b1_ref[0]
        glu = jnp.minimum(hh[:, :d_ff], SWIGLU_LIMIT)
        lin = jnp.clip(hh[:, d_ff:], -SWIGLU_LIMIT, SWIGLU_LIMIT)
        act = glu * (1.0 / (1.0 + jnp.exp(-SWIGLU_ALPHA * glu))) * (lin + 1.0)
        _store_row_tiles(ybuf_ref.at[slot], _dot(act.astype(BF16), w2b_ref[...]) + b2_ref[0])
        copy_out(j, slot).start(priority=1)
        return 0

    lax.fori_loop(0, n_blk, block, 0)

    for back in (1, 2):
        @pl.when(n_blk >= back)
        def _():
            copy_out(n_blk - back, (n_blk - back) % 2).wait()

    @pl.when(e == pl.num_programs(0) - 1)
    def _():
        ybuf_ref[0] = jnp.zeros(ybuf_ref.shape[1:], F32)
        _zero_unused_blocks(y_ref, ybuf_ref.at[0], sem_out.at[0], start_ref, cnt_ref)


def _experts(start, counts, xb, w1, b1, w2, b2):
    n_e, d, two_f = w1.shape
    d_ff = two_f // 2
    block_rows = EXPERT_TILE * SUBLANES
    any_spec = pl.BlockSpec(memory_space=pl.ANY)
    return pl.pallas_call(
        _experts_kernel,
        name="experts",
        grid_spec=pltpu.PrefetchScalarGridSpec(
            num_scalar_prefetch=2,
            grid=(n_e,),
            in_specs=[
                any_spec,
                pl.BlockSpec((1, d, two_f), lambda e, st, ct: (e, 0, 0)),
                pl.BlockSpec((1, 1, two_f), lambda e, st, ct: (e, 0, 0)),
                pl.BlockSpec((1, d_ff, d), lambda e, st, ct: (e, 0, 0)),
                pl.BlockSpec((1, 1, d), lambda e, st, ct: (e, 0, 0)),
            ],
            out_specs=any_spec,
            scratch_shapes=[
                pltpu.VMEM((d, two_f), BF16),
                pltpu.VMEM((d_ff, d), BF16),
                pltpu.VMEM((2, block_rows, LANES), F32),
                pltpu.VMEM((2, block_rows, LANES), F32),
                pltpu.SemaphoreType.DMA((2,)),
                pltpu.SemaphoreType.DMA((2,)),
            ],
        ),
        out_shape=jax.ShapeDtypeStruct(xb.shape, F32),
        compiler_params=pltpu.CompilerParams(
            dimension_semantics=("arbitrary",),
            vmem_limit_bytes=VMEM_LIMIT),
    )(start, counts, xb, w1, b1.reshape(n_e, 1, two_f), w2, b2.reshape(n_e, 1, d))


def _combine_kernel(dest_ref, x1_ref, w_ref, gf_ref, yb_ref, out_ref, buf_ref, sem,
                    *, n_tok):
    tile = x1_ref.shape[0]
    step = pl.program_id(0)
    slot = step % 2

    def gather(to_step, to_slot):
        def issue(t, _):
            for k in range(TOP_K):
                d = dest_ref[k * n_tok + to_step * tile + t]
                pltpu.make_async_copy(
                    yb_ref.at[pl.ds(pl.multiple_of(d * SUBLANES, SUBLANES), SUBLANES), :],
                    buf_ref.at[to_slot, pl.ds(pl.multiple_of((k * tile + t) * SUBLANES,
                                                              SUBLANES), SUBLANES), :],
                    sem.at[to_slot]).start(priority=k % 2)
            return 0

        lax.fori_loop(0, tile, issue, 0, unroll=8)

    @pl.when(step == 0)
    def _():
        gather(0, 0)

    @pl.when(step + 1 < pl.num_programs(0))
    def _():
        gather(step + 1, 1 - slot)

    pltpu.make_async_copy(yb_ref.at[pl.ds(0, TOP_K * tile * SUBLANES), :],
                          buf_ref.at[slot], sem.at[slot]).wait()
    acc = x1_ref[...]
    w = w_ref[...]
    rows = buf_ref.at[slot]
    for k in range(TOP_K):
        acc = acc + _load_row_tiles(rows, tile, k * tile * SUBLANES) * w[:, k:k + 1]
    out_ref[...] = _rms(acc, gf_ref[...])


def _combine(dest_flat, x1, w_tok, gf, yb):
    n, d = x1.shape
    tile = min(COMBINE_TILE, n)
    return pl.pallas_call(
        functools.partial(_combine_kernel, n_tok=n),
        name="combine",
        grid_spec=pltpu.PrefetchScalarGridSpec(
            num_scalar_prefetch=1,
            grid=(n // tile,),
            in_specs=[
                pl.BlockSpec((tile, d), lambda i, dr: (i, 0)),
                pl.BlockSpec((tile, TOP_K), lambda i, dr: (i, 0)),
                pl.BlockSpec((1, d), lambda i, dr: (0, 0)),
                pl.BlockSpec(memory_space=pl.ANY),
            ],
            out_specs=pl.BlockSpec((tile, d), lambda i, dr: (i, 0)),
            scratch_shapes=[pltpu.VMEM((2, TOP_K * tile * SUBLANES, LANES), F32),
                            pltpu.SemaphoreType.DMA((2,))],
        ),
        out_shape=jax.ShapeDtypeStruct((n, d), F32),
        compiler_params=pltpu.CompilerParams(
            dimension_semantics=("arbitrary",),
            vmem_limit_bytes=VMEM_LIMIT),
    )(dest_flat, x1, w_tok, gf, yb)


def kernel(x, norm1_g, w_in, b_f, g_fox, g_sb, w_out, norm2_g, w_router, b_router,
           w1, b1, w2, b2, norm_f_g):
    b, s, d = x.shape
    n = b * s
    depth = norm1_g.shape[0]
    scale = HEAD_DIM ** -0.5 * LOG2E
    fw, sw = FOX_WIDTH, SB_WIDTH
    n_blocks = -(-n * TOP_K // EXPERT_TILE) + N_EXPERTS
    n_rows = n_blocks * EXPERT_TILE
    assert depth == 1, "the combine kernel fuses the final norm into the only layer"
    assert d == SUBLANES * LANES, "row-tile layout holds one model row per (8, 128) tile"
    wl = w_in[0]
    f0 = 3 * fw
    s0 = f0 + N_FOX_HEADS
    w_qk = jnp.concatenate(
        [wl[:, :fw] * scale, wl[:, s0:s0 + sw] * scale,
         wl[:, fw:2 * fw], wl[:, s0 + sw:s0 + 2 * sw]], axis=1).astype(BF16)
    w_vt = jnp.concatenate([wl[:, 2 * fw:f0], wl[:, s0 + 2 * sw:]], axis=1).T.astype(BF16)
    w_f = jnp.pad(wl[:, f0:s0], ((0, 0), (0, LANES - N_FOX_HEADS))).astype(BF16)
    bias_f = jnp.pad(b_f[0], (0, LANES - N_FOX_HEADS))[None, :]
    q, kf, ks, aug, vt = _in_proj(x, norm1_g[0][None, :], w_qk, w_vt, w_f, bias_f,
                                  _aug_selector())
    fox = _fox(q, kf, aug, vt)
    sb = _stick(q, ks, vt)
    x1, h2, e_k, w_k, rank_k, counts = _out_proj(
        x.reshape(n, d), fox.reshape(n, fw), sb.reshape(n, sw),
        g_fox[0][None, :], g_sb[0][None, :], w_out[0].astype(BF16),
        norm2_g[0][None, :], w_router[0].T, b_router[0][:, None])
    counts = counts[:, 0].astype(jnp.int32)
    dest, start = _route(counts, e_k, rank_k)
    dest_flat = dest.reshape(-1)
    xb = _dispatch(dest_flat, start, counts, h2, n_rows)
    yb = _experts(start, counts, xb, w1[0], b1[0], w2[0], b2[0])
    out = _combine(dest_flat, x1, w_k.T, norm_f_g[None, :], yb)
    return out.reshape(b, s, d)
```

```python
import functools

import jax
import jax.numpy as jnp
import numpy as np
from jax import lax
from jax.experimental import pallas as pl
from jax.experimental.pallas import tpu as pltpu

HEAD_DIM = 64
N_FOX_HEADS = 8
N_SB_HEADS = 8
FOX_WIDTH = N_FOX_HEADS * HEAD_DIM
SB_WIDTH = N_SB_HEADS * HEAD_DIM
N_EXPERTS = 32
TOP_K = 4
SWIGLU_LIMIT = 7.0
SWIGLU_ALPHA = 1.702
RMS_EPS = 1e-5
LOG2E = 1.4426950408889634

LANES = 128
SUBLANES = 8
PAIRS = FOX_WIDTH // LANES

ROW_TILE = 256
OUT_TILE = 1024
Q_TILE = 512
CUM_CHUNK = 256
KEY_CHUNK = 64
EXPERT_TILE = 256
DISPATCH_TILE = 1024
COMBINE_TILE = 512
VMEM_LIMIT = 56 * 1024 * 1024

F32 = jnp.float32
BF16 = jnp.bfloat16


def _dot(a, b):
    return jnp.dot(a, b, preferred_element_type=F32)


def _dot_nt(a, b):
    return lax.dot_general(a, b, (((1,), (1,)), ((), ())), preferred_element_type=F32)


def _split2(v):
    hi = v.astype(BF16)
    lo = (v - hi.astype(F32)).astype(BF16)
    return hi, lo


def _split3(v):
    hi = v.astype(BF16)
    r = v - hi.astype(F32)
    mid = r.astype(BF16)
    lo = (r - mid.astype(F32)).astype(BF16)
    return hi, mid, lo


def _softplus(z):
    return jnp.maximum(z, 0.0) + jnp.log1p(jnp.exp(-jnp.abs(z)))


def _softplus2(z2):
    sign_bit = jnp.uint32(0x80000000)
    neg_abs = lax.bitcast_convert_type(lax.bitcast_convert_type(z2, jnp.uint32) | sign_bit, F32)
    return jnp.maximum(z2, 0.0) + jnp.log2(1.0 + jnp.exp2(neg_abs))


def _load_row_tiles(ref, rows, first=0):
    return jnp.concatenate(
        [ref[pl.ds(first + c, rows, stride=SUBLANES), :] for c in range(SUBLANES)], axis=1)


def _store_row_tiles(ref, value):
    rows = value.shape[0]
    for c in range(SUBLANES):
        ref[pl.ds(c, rows, stride=SUBLANES), :] = value[:, c * LANES:(c + 1) * LANES]


def _rms(v, g):
    return v * lax.rsqrt(jnp.mean(v * v, axis=-1, keepdims=True) + RMS_EPS) * g


def _in_proj_kernel(x_ref, g_ref, wqk_ref, wvt_ref, wf_ref, bf_ref, sel_ref,
                    q_ref, kf_ref, ks_ref, aug_ref, vt_ref, carry_ref):
    @pl.when(pl.program_id(1) == 0)
    def _():
        carry_ref[...] = jnp.zeros_like(carry_ref)

    tm = x_ref.shape[1]
    mix = FOX_WIDTH + SB_WIDTH
    hb = _rms(x_ref[0], g_ref[...]).astype(BF16)
    qk = _dot(hb, wqk_ref[...])
    q_ref[0] = qk[:, :mix].astype(BF16)
    kf_ref[0] = qk[:, mix:mix + FOX_WIDTH].astype(BF16)
    ks_ref[0] = qk[:, mix + FOX_WIDTH:].astype(BF16)
    vt_ref[0] = _dot_nt(wvt_ref[...], hb).astype(BF16)
    log_f = -LOG2E * _softplus(-(_dot(hb, wf_ref[...]) + bf_ref[...]))
    row = lax.broadcasted_iota(jnp.int32, (tm, tm), 0)
    col = lax.broadcasted_iota(jnp.int32, (tm, tm), 1)
    tri = (col <= row).astype(BF16)
    hi, mid, lo = _split3(log_f)
    c = _dot(tri, hi) + _dot(tri, mid) + _dot(tri, lo) + carry_ref[...]
    carry_ref[...] = c[tm - 1:tm, :]
    c_hi, c_mid, c_lo = _split3(c)
    aug = _dot(c_hi, sel_ref[0]) + _dot(c_mid, sel_ref[1]) + _dot(c_lo, sel_ref[2])
    aug_ref[0] = aug.astype(BF16)


def _aug_selector():
    sel = np.zeros((3, LANES, FOX_WIDTH), np.float32)
    for t in range(3):
        for h in range(N_FOX_HEADS):
            sel[t, h, LANES * (h // 2) + 3 * (h % 2) + t] = -1.0
    return jnp.asarray(sel, BF16)


def _in_proj(x, g, wqk, wvt, wf, bf, sel):
    b, s, d = x.shape
    mix = FOX_WIDTH + SB_WIDTH
    tm = min(ROW_TILE, s)
    const2 = lambda i, j: (0, 0)
    return pl.pallas_call(
        _in_proj_kernel,
        name="in_proj",
        grid=(b, s // tm),
        in_specs=[
            pl.BlockSpec((1, tm, d), lambda i, j: (i, j, 0)),
            pl.BlockSpec((1, d), const2),
            pl.BlockSpec((d, 2 * mix), const2),
            pl.BlockSpec((mix, d), const2),
            pl.BlockSpec((d, LANES), const2),
            pl.BlockSpec((1, LANES), const2),
            pl.BlockSpec((3, LANES, FOX_WIDTH), lambda i, j: (0, 0, 0)),
        ],
        out_specs=[
            pl.BlockSpec((1, tm, mix), lambda i, j: (i, j, 0)),
            pl.BlockSpec((1, tm, FOX_WIDTH), lambda i, j: (i, j, 0)),
            pl.BlockSpec((1, tm, SB_WIDTH), lambda i, j: (i, j, 0)),
            pl.BlockSpec((1, tm, FOX_WIDTH), lambda i, j: (i, j, 0)),
            pl.BlockSpec((1, mix, tm), lambda i, j: (i, 0, j)),
        ],
        out_shape=[
            jax.ShapeDtypeStruct((b, s, mix), BF16),
            jax.ShapeDtypeStruct((b, s, FOX_WIDTH), BF16),
            jax.ShapeDtypeStruct((b, s, SB_WIDTH), BF16),
            jax.ShapeDtypeStruct((b, s, FOX_WIDTH), BF16),
            jax.ShapeDtypeStruct((b, mix, s), BF16),
        ],
        scratch_shapes=[pltpu.VMEM((1, LANES), F32)],
        compiler_params=pltpu.CompilerParams(
            dimension_semantics=("parallel", "arbitrary"),
            vmem_limit_bytes=VMEM_LIMIT),
    )(x, g, wqk, wvt, wf, bf, sel)


def _stack_heads(q, ones_cols):
    tq = q.shape[0]
    lane = lax.broadcasted_iota(jnp.int32, (tq, LANES), 1)
    low = lane < HEAD_DIM
    qf = q.astype(F32)
    halves = [jnp.where(low, qf, 0.0), jnp.where(low, 0.0, qf)]
    if ones_cols:
        halves = [jnp.concatenate([halves[hh], jnp.where(
            (lane >= 3 * hh) & (lane < 3 * hh + 3), 1.0, 0.0)], axis=1) for hh in range(2)]
    return jnp.concatenate(halves, axis=0).astype(q.dtype)


def _unstack_heads(acc_t, tq):
    sub = lax.broadcasted_iota(jnp.int32, (LANES, tq), 0)
    return jnp.where(sub < HEAD_DIM, acc_t[:, :tq], acc_t[:, tq:]).T


def _fox_kernel(*refs, tq):
    n_q = refs[0].shape[1] // tq
    lax.fori_loop(0, n_q, lambda i, _: _fox_query_block(i, tq, *refs), 0)


def _fox_query_block(i, tq, q_ref, k_ref, aug_ref, vt_ref, o_ref,
                     q2_ref, m_ref, l_ref, acc_ref, s_ref, p_ref, a_ref):
    q_rows = pl.ds(pl.multiple_of(i * tq, tq), tq)
    q2_ref[...] = _stack_heads(q_ref[0, q_rows, :], True)
    m_ref[...] = jnp.full(m_ref.shape, -1e30, F32)
    l_ref[...] = jnp.zeros_like(l_ref)
    acc_ref[...] = jnp.zeros_like(acc_ref)

    def scores(j, buf):
        k0 = pl.multiple_of(j * tq, tq)
        ka = jnp.concatenate(
            [k_ref[0, pl.ds(k0, tq), :], aug_ref[0, pl.ds(k0, tq), :]], axis=1)
        s_ref[buf] = _dot_nt(ka, q2_ref[...])

    def softmax_update(buf, diagonal):
        if diagonal:
            key = lax.broadcasted_iota(jnp.int32, (tq, 2 * tq), 0)
            qry = lax.broadcasted_iota(jnp.int32, (tq, 2 * tq), 1)
            qry = jnp.where(qry >= tq, qry - tq, qry)
            s_ref[buf] = jnp.where(key <= qry, s_ref[buf], -jnp.inf)
        m_old = m_ref[...]
        m_new = jnp.maximum(m_old, jnp.max(s_ref[buf], axis=0, keepdims=True))
        alpha = jnp.exp2(m_old - m_new)
        m_ref[...] = m_new
        a_ref[buf] = alpha
        part = jnp.zeros((SUBLANES, 2 * tq), F32)
        m_rows = jnp.broadcast_to(m_new, (SUBLANES, 2 * tq))
        for r in range(0, tq, KEY_CHUNK):
            sc = s_ref[buf, r:r + KEY_CHUNK, :].reshape(
                KEY_CHUNK // SUBLANES, SUBLANES, 2 * tq)
            pr = jnp.exp2(sc - m_rows)
            part = part + jnp.sum(pr, axis=0)
            p_ref[buf, r:r + KEY_CHUNK, :] = pr.reshape(KEY_CHUNK, 2 * tq).astype(BF16)
        l_ref[...] = l_ref[...] * alpha + jnp.sum(part, axis=0, keepdims=True)

    def accumulate(j, buf):
        k0 = pl.multiple_of(j * tq, tq)
        acc_ref[...] = (acc_ref[...] * a_ref[buf]
                        + _dot(vt_ref[0, :, pl.ds(k0, tq)], p_ref[buf]))

    def step(j, buf):
        scores(j + 1, 1 - buf)
        softmax_update(buf, False)
        accumulate(jnp.maximum(j - 1, 0), 1 - buf)

    def last(buf):
        softmax_update(buf, True)
        accumulate(jnp.maximum(i - 1, 0), 1 - buf)
        accumulate(i, buf)

    scores(0, 0)
    p_ref[1] = jnp.zeros(p_ref.shape[1:], BF16)
    a_ref[1] = jnp.ones(a_ref.shape[1:], F32)

    def two_steps(t, _):
        step(2 * t, 0)
        step(2 * t + 1, 1)
        return 0

    lax.fori_loop(0, i // 2, two_steps, 0)

    @pl.when(i % 2 == 1)
    def _():
        step(i - 1, 0)
        last(1)

    @pl.when(i % 2 == 0)
    def _():
        last(0)

    o_ref[0, q_rows, :] = _unstack_heads(acc_ref[...] / l_ref[...], tq)
    return 0


def _fox(q, kf, aug, vt):
    b, s, _ = q.shape
    tq = min(Q_TILE, s)
    return pl.pallas_call(
        functools.partial(_fox_kernel, tq=tq),
        name="fox",
        grid=(b, PAIRS),
        in_specs=[
            pl.BlockSpec((1, s, LANES), lambda bi, p: (bi, 0, p)),
            pl.BlockSpec((1, s, LANES), lambda bi, p: (bi, 0, p)),
            pl.BlockSpec((1, s, LANES), lambda bi, p: (bi, 0, p)),
            pl.BlockSpec((1, LANES, s), lambda bi, p: (bi, p, 0)),
        ],
        out_specs=pl.BlockSpec((1, s, LANES), lambda bi, p: (bi, 0, p)),
        out_shape=jax.ShapeDtypeStruct((b, s, FOX_WIDTH), F32),
        scratch_shapes=[
            pltpu.VMEM((2 * tq, 2 * LANES), BF16),
            pltpu.VMEM((1, 2 * tq), F32),
            pltpu.VMEM((1, 2 * tq), F32),
            pltpu.VMEM((LANES, 2 * tq), F32),
            pltpu.VMEM((2, tq, 2 * tq), F32),
            pltpu.VMEM((2, tq, 2 * tq), BF16),
            pltpu.VMEM((2, 1, 2 * tq), F32),
        ],
        compiler_params=pltpu.CompilerParams(
            dimension_semantics=("parallel", "parallel"),
            vmem_limit_bytes=VMEM_LIMIT),
    )(q, kf, aug, vt)


def _stick_kernel(*refs, tq):
    n_q = refs[0].shape[1] // tq
    lax.fori_loop(0, n_q, lambda i, _: _stick_query_block(i, tq, *refs), 0)


def _stick_query_block(i, tq, q_ref, k_ref, vt_ref, o_ref,
                       q2_ref, tail_ref, acc_ref, z_ref, a_ref):
    tc = min(CUM_CHUNK, tq)
    q_rows = pl.ds(pl.multiple_of(i * tq, tq), tq)
    q2_ref[...] = _stack_heads(q_ref[0, q_rows, :], False)
    tail_ref[...] = jnp.zeros_like(tail_ref)
    acc_ref[...] = jnp.zeros_like(acc_ref)
    kr = lax.broadcasted_iota(jnp.int32, (tc, tc), 0)
    kc = lax.broadcasted_iota(jnp.int32, (tc, tc), 1)
    later = (kc > kr).astype(BF16)

    def scores(j, buf):
        k0 = pl.multiple_of(j * tq, tq)
        z_ref[buf] = _dot_nt(k_ref[0, pl.ds(k0, tq), :], q2_ref[...])

    def weights(buf, diagonal):
        tail = tail_ref[...]
        for ch in reversed(range(tq // tc)):
            z_c = z_ref[buf, ch * tc:(ch + 1) * tc, :]
            sp_c = _softplus2(z_c)
            if diagonal:
                key = lax.broadcasted_iota(jnp.int32, (tc, 2 * tq), 0) + ch * tc
                qry = lax.broadcasted_iota(jnp.int32, (tc, 2 * tq), 1)
                qry = jnp.where(qry >= tq, qry - tq, qry)
                strict = key < qry
                sp_m = jnp.where(strict, sp_c, 0.0)
            else:
                sp_m = sp_c
            suffix = _dot(later, sp_m.astype(BF16))
            a = jnp.exp2((z_c - sp_c) - (suffix + tail))
            if diagonal:
                a = jnp.where(strict, a, 0.0)
            a_ref[buf, ch * tc:(ch + 1) * tc, :] = a.astype(BF16)
            tail = tail + suffix[0:1, :] + sp_m[0:1, :]
        tail_ref[...] = tail

    def accumulate(j, buf):
        k0 = pl.multiple_of(j * tq, tq)
        acc_ref[...] = acc_ref[...] + _dot(vt_ref[0, :, pl.ds(k0, tq)], a_ref[buf])

    def step(n, buf, diagonal):
        j = i - n
        scores(j - 1, 1 - buf)
        weights(buf, diagonal)
        if not diagonal:
            accumulate(j + 1, 1 - buf)

    def last(buf, diagonal):
        weights(buf, diagonal)
        if not diagonal:
            accumulate(1, 1 - buf)
        accumulate(0, buf)

    scores(i, 0)

    @pl.when(i == 0)
    def _():
        last(0, True)

    @pl.when(i > 0)
    def _():
        step(0, 0, True)

        def two_steps(t, _):
            step(1 + 2 * t, 1, False)
            step(2 + 2 * t, 0, False)
            return 0

        lax.fori_loop(0, (i - 1) // 2, two_steps, 0)

        @pl.when(i % 2 == 0)
        def _():
            step(i - 1, 1, False)
            last(0, False)

        @pl.when(i % 2 == 1)
        def _():
            last(1, False)

    o_ref[0, q_rows, :] = _unstack_heads(acc_ref[...], tq)
    return 0


def _stick(q, ks, vt):
    b, s, _ = q.shape
    tq = min(Q_TILE, s)
    return pl.pallas_call(
        functools.partial(_stick_kernel, tq=tq),
        name="stick",
        grid=(b, PAIRS),
        in_specs=[
            pl.BlockSpec((1, s, LANES), lambda bi, p: (bi, 0, PAIRS + p)),
            pl.BlockSpec((1, s, LANES), lambda bi, p: (bi, 0, p)),
            pl.BlockSpec((1, LANES, s), lambda bi, p: (bi, PAIRS + p, 0)),
        ],
        out_specs=pl.BlockSpec((1, s, LANES), lambda bi, p: (bi, 0, p)),
        out_shape=jax.ShapeDtypeStruct((b, s, SB_WIDTH), F32),
        scratch_shapes=[
            pltpu.VMEM((2 * tq, LANES), BF16),
            pltpu.VMEM((1, 2 * tq), F32),
            pltpu.VMEM((LANES, 2 * tq), F32),
            pltpu.VMEM((2, tq, 2 * tq), F32),
            pltpu.VMEM((2, tq, 2 * tq), BF16),
        ],
        compiler_params=pltpu.CompilerParams(
            dimension_semantics=("parallel", "parallel"),
            vmem_limit_bytes=VMEM_LIMIT),
    )(q, ks, vt)


def _out_proj_kernel(x_ref, fox_ref, sb_ref, gfox_ref, gsb_ref, wo_ref, g2_ref,
                     wr_ref, br_ref,
                     x1_ref, h2_ref, e_ref, w_ref, rank_ref, cnt_ref, carry_ref):
    @pl.when(pl.program_id(0) == 0)
    def _():
        carry_ref[...] = jnp.zeros_like(carry_ref)

    tm = x_ref.shape[0]
    a = _rms(fox_ref[...], gfox_ref[...]).astype(BF16)
    bb = _rms(sb_ref[...], gsb_ref[...]).astype(BF16)
    x1 = x_ref[...] + _dot(a, wo_ref[0:FOX_WIDTH, :]) + _dot(bb, wo_ref[FOX_WIDTH:, :])
    x1_ref[...] = x1
    h2 = _rms(x1, g2_ref[...])
    _store_row_tiles(h2_ref, h2)
    h_hi, h_lo = _split2(h2)
    w_hi, w_lo = _split2(wr_ref[...])
    logits = (_dot_nt(w_hi, h_hi) + _dot_nt(w_hi, h_lo) + _dot_nt(w_lo, h_hi)
              + br_ref[...])
    e_iota = lax.broadcasted_iota(jnp.int32, (N_EXPERTS, tm), 0)
    work = logits
    vals, idxs, sels = [], [], []
    for _ in range(TOP_K):
        m = jnp.max(work, axis=0, keepdims=True)
        idx = jnp.min(jnp.where(work == m, e_iota, N_EXPERTS), axis=0, keepdims=True)
        sel = e_iota == idx
        work = jnp.where(sel, -jnp.inf, work)
        vals.append(m)
        idxs.append(idx)
        sels.append(sel)
    exps = [jnp.exp(v - vals[0]) for v in vals]
    denom = exps[0] + exps[1] + exps[2] + exps[3]
    e_ref[...] = jnp.concatenate(idxs, axis=0)
    w_ref[...] = jnp.concatenate([ex / denom for ex in exps], axis=0)
    chosen = jnp.where(sels[0] | sels[1] | sels[2] | sels[3], 1.0, 0.0)
    row = lax.broadcasted_iota(jnp.int32, (tm, tm), 0)
    col = lax.broadcasted_iota(jnp.int32, (tm, tm), 1)
    before = (row < col).astype(BF16)
    rank = _dot(chosen.astype(BF16), before) + carry_ref[...]
    rank_ref[...] = jnp.concatenate(
        [jnp.sum(jnp.where(sel, rank, 0.0), axis=0, keepdims=True) for sel in sels],
        axis=0).astype(jnp.int32)
    total = carry_ref[...] + jnp.sum(chosen, axis=1, keepdims=True)
    carry_ref[...] = total
    cnt_ref[...] = jnp.broadcast_to(total, cnt_ref.shape)


def _out_proj(x2, fox2, sb2, gfox, gsb, wo, g2, wr_t, br):
    n, d = x2.shape
    tm = min(OUT_TILE, n)
    const = lambda i: (0, 0)
    return pl.pallas_call(
        _out_proj_kernel,
        name="out_proj",
        grid=(n // tm,),
        in_specs=[
            pl.BlockSpec((tm, d), lambda i: (i, 0)),
            pl.BlockSpec((tm, FOX_WIDTH), lambda i: (i, 0)),
            pl.BlockSpec((tm, SB_WIDTH), lambda i: (i, 0)),
            pl.BlockSpec((1, FOX_WIDTH), const),
            pl.BlockSpec((1, SB_WIDTH), const),
            pl.BlockSpec((FOX_WIDTH + SB_WIDTH, d), const),
            pl.BlockSpec((1, d), const),
            pl.BlockSpec((N_EXPERTS, d), const),
            pl.BlockSpec((N_EXPERTS, 1), const),
        ],
        out_specs=[
            pl.BlockSpec((tm, d), lambda i: (i, 0)),
            pl.BlockSpec((tm * SUBLANES, LANES), lambda i: (i, 0)),
            pl.BlockSpec((TOP_K, tm), lambda i: (0, i)),
            pl.BlockSpec((TOP_K, tm), lambda i: (0, i)),
            pl.BlockSpec((TOP_K, tm), lambda i: (0, i)),
            pl.BlockSpec((N_EXPERTS, LANES), const),
        ],
        out_shape=[
            jax.ShapeDtypeStruct((n, d), F32),
            jax.ShapeDtypeStruct((n * SUBLANES, LANES), F32),
            jax.ShapeDtypeStruct((TOP_K, n), jnp.int32),
            jax.ShapeDtypeStruct((TOP_K, n), F32),
            jax.ShapeDtypeStruct((TOP_K, n), jnp.int32),
            jax.ShapeDtypeStruct((N_EXPERTS, LANES), F32),
        ],
        scratch_shapes=[pltpu.VMEM((N_EXPERTS, 1), F32)],
        compiler_params=pltpu.CompilerParams(
            dimension_semantics=("arbitrary",),
            vmem_limit_bytes=VMEM_LIMIT),
    )(x2, fox2, sb2, gfox, gsb, wo, g2, wr_t, br)


def _route_kernel(cnt_ref, e_ref, rank_ref, dest_ref, start_ref):
    def per_expert(e, pstart):
        start_ref[e] = pstart
        return pstart + (cnt_ref[e] + EXPERT_TILE - 1) // EXPERT_TILE * EXPERT_TILE

    lax.fori_loop(0, N_EXPERTS, per_expert, 0)
    ev = e_ref[...]
    offs = jnp.zeros(ev.shape, jnp.int32)
    for e in range(N_EXPERTS):
        offs = jnp.where(ev == e, start_ref[e], offs)
    dest_ref[...] = rank_ref[...] + offs


def _route(counts, e_k, rank_k):
    smem = pl.BlockSpec(memory_space=pltpu.SMEM)
    vmem = pl.BlockSpec(memory_space=pltpu.VMEM)
    return pl.pallas_call(
        _route_kernel,
        name="route",
        in_specs=[smem, vmem, vmem],
        out_specs=[vmem, smem],
        out_shape=[
            jax.ShapeDtypeStruct(e_k.shape, jnp.int32),
            jax.ShapeDtypeStruct((N_EXPERTS,), jnp.int32),
        ],
    )(counts, e_k, rank_k)


def _zero_unused_blocks(dst_ref, zero_block_ref, sem, start_ref, cnt_ref):
    block_rows = zero_block_ref.shape[0]
    last = N_EXPERTS - 1
    used = (start_ref[last] + (cnt_ref[last] + EXPERT_TILE - 1) // EXPERT_TILE
            * EXPERT_TILE) * SUBLANES
    count = (dst_ref.shape[0] - used) // block_rows

    def copy(b):
        rows = pl.ds(pl.multiple_of(used + b * block_rows, block_rows), block_rows)
        return pltpu.make_async_copy(zero_block_ref, dst_ref.at[rows, :], sem)

    def start(b, _):
        copy(b).start()
        return 0

    def wait(b, _):
        copy(b).wait()
        return 0

    lax.fori_loop(0, count, start, 0)
    lax.fori_loop(0, count, wait, 0)


def _dispatch_kernel(dest_ref, start_ref, cnt_ref, h_ref, xb_ref, zero_ref, sem, pad_sem,
                     *, n_tok, tile):
    base = pl.program_id(0) * tile

    @pl.when(pl.program_id(0) == 0)
    def _():
        zero_ref[...] = jnp.zeros_like(zero_ref)

        def pad_rows(e, wait):
            n_pad = (-cnt_ref[e]) % EXPERT_TILE
            pos = start_ref[e] + cnt_ref[e]
            size = EXPERT_TILE // 2
            while size >= 1:
                @pl.when((n_pad & size) != 0)
                def _(pos=pos, size=size):
                    row = pl.multiple_of(pos * SUBLANES, SUBLANES)
                    copy = pltpu.make_async_copy(
                        zero_ref.at[pl.ds(0, size * SUBLANES), :],
                        xb_ref.at[pl.ds(row, size * SUBLANES), :], pad_sem)
                    if wait:
                        copy.wait()
                    else:
                        copy.start()

                pos = pos + (n_pad & size)
                size //= 2
            return 0

        lax.fori_loop(0, N_EXPERTS, lambda e, _: pad_rows(e, False), 0)
        lax.fori_loop(0, N_EXPERTS, lambda e, _: pad_rows(e, True), 0)
        _zero_unused_blocks(xb_ref, zero_ref, pad_sem, start_ref, cnt_ref)

    def issue(t, _):
        src = h_ref.at[pl.ds(pl.multiple_of(t * SUBLANES, SUBLANES), SUBLANES), :]
        for k in range(TOP_K):
            d = dest_ref[k * n_tok + base + t]
            pltpu.make_async_copy(
                src, xb_ref.at[pl.ds(pl.multiple_of(d * SUBLANES, SUBLANES), SUBLANES), :],
                sem).start(priority=k % 2)
        return 0

    lax.fori_loop(0, tile, issue, 0, unroll=8)
    rows = xb_ref.at[pl.ds(0, TOP_K * tile * SUBLANES), :]
    pltpu.make_async_copy(rows, rows, sem).wait()


def _dispatch(dest_flat, start, counts, h2, n_rows):
    n = h2.shape[0] // SUBLANES
    tile = min(DISPATCH_TILE, n)
    return pl.pallas_call(
        functools.partial(_dispatch_kernel, n_tok=n, tile=tile),
        name="dispatch",
        grid_spec=pltpu.PrefetchScalarGridSpec(
            num_scalar_prefetch=3,
            grid=(n // tile,),
            in_specs=[pl.BlockSpec((tile * SUBLANES, LANES), lambda i, dr, st, ct: (i, 0))],
            out_specs=pl.BlockSpec(memory_space=pl.ANY),
            scratch_shapes=[pltpu.VMEM((EXPERT_TILE * SUBLANES, LANES), F32),
                            pltpu.SemaphoreType.DMA(()),
                            pltpu.SemaphoreType.DMA(())],
        ),
        out_shape=jax.ShapeDtypeStruct((n_rows * SUBLANES, LANES), h2.dtype),
        compiler_params=pltpu.CompilerParams(dimension_semantics=("arbitrary",)),
    )(dest_flat, start, counts, h2)


def _experts_kernel(start_ref, cnt_ref, xb_ref, w1_ref, b1_ref, w2_ref, b2_ref, y_ref,
                    w1b_ref, w2b_ref, xbuf_ref, ybuf_ref, sem_in, sem_out):
    e = pl.program_id(0)
    d_ff = w2_ref.shape[1]
    block_rows = EXPERT_TILE * SUBLANES
    n_blk = (cnt_ref[e] + EXPERT_TILE - 1) // EXPERT_TILE
    first = start_ref[e] * SUBLANES

    def rows(j):
        return pl.ds(pl.multiple_of(first + j * block_rows, block_rows), block_rows)

    def copy_in(j, slot):
        return pltpu.make_async_copy(xb_ref.at[rows(j), :], xbuf_ref.at[slot], sem_in.at[slot])

    def copy_out(j, slot):
        return pltpu.make_async_copy(ybuf_ref.at[slot], y_ref.at[rows(j), :], sem_out.at[slot])

    @pl.when(n_blk > 0)
    def _():
        copy_in(0, 0).start(priority=1)

    w1b_ref[...] = w1_ref[0].astype(BF16)
    w2b_ref[...] = w2_ref[0].astype(BF16)

    def block(j, _):
        slot = j % 2

        @pl.when(j + 1 < n_blk)
        def _():
            copy_in(j + 1, 1 - slot).start(priority=1)

        copy_in(j, slot).wait()

        @pl.when(j >= 2)
        def _():
            copy_out(j - 2, slot).wait()

        x = _load_row_tiles(xbuf_ref.at[slot], EXPERT_TILE).astype(BF16)
        hh = _dot(x, w1b_ref[...]) + b1_ref[0]
        glu = jnp.minimum(hh[:, :d_ff], SWIGLU_LIMIT)
        lin = jnp.clip(hh[:, d_ff:], -SWIGLU_LIMIT, SWIGLU_LIMIT)
        act = glu * (1.0 / (1.0 + jnp.exp(-SWIGLU_ALPHA * glu))) * (lin + 1.0)
        _store_row_tiles(ybuf_ref.at[slot], _dot(act.astype(BF16), w2b_ref[...]) + b2_ref[0])
        copy_out(j, slot).start(priority=1)
        return 0

    lax.fori_loop(0, n_blk, block, 0)

    for back in (1, 2):
        @pl.when(n_blk >= back)
        def _():
            copy_out(n_blk - back, (n_blk - back) % 2).wait()

    @pl.when(e == pl.num_programs(0) - 1)
    def _():
        ybuf_ref[0] = jnp.zeros(ybuf_ref.shape[1:], F32)
        _zero_unused_blocks(y_ref, ybuf_ref.at[0], sem_out.at[0], start_ref, cnt_ref)


def _experts(start, counts, xb, w1, b1, w2, b2):
    n_e, d, two_f = w1.shape
    d_ff = two_f // 2
    block_rows = EXPERT_TILE * SUBLANES
    any_spec = pl.BlockSpec(memory_space=pl.ANY)
    return pl.pallas_call(
        _experts_kernel,
        name="experts",
        grid_spec=pltpu.PrefetchScalarGridSpec(
            num_scalar_prefetch=2,
            grid=(n_e,),
            in_specs=[
                any_spec,
                pl.BlockSpec((1, d, two_f), lambda e, st, ct: (e, 0, 0)),
                pl.BlockSpec((1, 1, two_f), lambda e, st, ct: (e, 0, 0)),
                pl.BlockSpec((1, d_ff, d), lambda e, st, ct: (e, 0, 0)),
                pl.BlockSpec((1, 1, d), lambda e, st, ct: (e, 0, 0)),
            ],
            out_specs=any_spec,
            scratch_shapes=[
                pltpu.VMEM((d, two_f), BF16),
                pltpu.VMEM((d_ff, d), BF16),
                pltpu.VMEM((2, block_rows, LANES), F32),
                pltpu.VMEM((2, block_rows, LANES), F32),
                pltpu.SemaphoreType.DMA((2,)),
                pltpu.SemaphoreType.DMA((2,)),
            ],
        ),
        out_shape=jax.ShapeDtypeStruct(xb.shape, F32),
        compiler_params=pltpu.CompilerParams(
            dimension_semantics=("arbitrary",),
            vmem_limit_bytes=VMEM_LIMIT),
    )(start, counts, xb, w1, b1.reshape(n_e, 1, two_f), w2, b2.reshape(n_e, 1, d))


def _combine_kernel(dest_ref, x1_ref, w_ref, gf_ref, yb_ref, out_ref, buf_ref, sem,
                    *, n_tok):
    tile = x1_ref.shape[0]
    step = pl.program_id(0)
    slot = step % 2

    def gather(to_step, to_slot):
        def issue(t, _):
            for k in range(TOP_K):
                d = dest_ref[k * n_tok + to_step * tile + t]
                pltpu.make_async_copy(
                    yb_ref.at[pl.ds(pl.multiple_of(d * SUBLANES, SUBLANES), SUBLANES), :],
                    buf_ref.at[to_slot, pl.ds(pl.multiple_of((k * tile + t) * SUBLANES,
                                                              SUBLANES), SUBLANES), :],
                    sem.at[to_slot]).start(priority=k % 2)
            return 0

        lax.fori_loop(0, tile, issue, 0, unroll=8)

    @pl.when(step == 0)
    def _():
        gather(0, 0)

    @pl.when(step + 1 < pl.num_programs(0))
    def _():
        gather(step + 1, 1 - slot)

    pltpu.make_async_copy(yb_ref.at[pl.ds(0, TOP_K * tile * SUBLANES), :],
                          buf_ref.at[slot], sem.at[slot]).wait()
    acc = x1_ref[...]
    w = w_ref[...]
    rows = buf_ref.at[slot]
    for k in range(TOP_K):
        acc = acc + _load_row_tiles(rows, tile, k * tile * SUBLANES) * w[:, k:k + 1]
    out_ref[...] = _rms(acc, gf_ref[...])


def _combine(dest_flat, x1, w_tok, gf, yb):
    n, d = x1.shape
    tile = min(COMBINE_TILE, n)
    return pl.pallas_call(
        functools.partial(_combine_kernel, n_tok=n),
        name="combine",
        grid_spec=pltpu.PrefetchScalarGridSpec(
            num_scalar_prefetch=1,
            grid=(n // tile,),
            in_specs=[
                pl.BlockSpec((tile, d), lambda i, dr: (i, 0)),
                pl.BlockSpec((tile, TOP_K), lambda i, dr: (i, 0)),
                pl.BlockSpec((1, d), lambda i, dr: (0, 0)),
                pl.BlockSpec(memory_space=pl.ANY),
            ],
            out_specs=pl.BlockSpec((tile, d), lambda i, dr: (i, 0)),
            scratch_shapes=[pltpu.VMEM((2, TOP_K * tile * SUBLANES, LANES), F32),
                            pltpu.SemaphoreType.DMA((2,))],
        ),
        out_shape=jax.ShapeDtypeStruct((n, d), F32),
        compiler_params=pltpu.CompilerParams(
            dimension_semantics=("arbitrary",),
            vmem_limit_bytes=VMEM_LIMIT),
    )(dest_flat, x1, w_tok, gf, yb)


def kernel(x, norm1_g, w_in, b_f, g_fox, g_sb, w_out, norm2_g, w_router, b_router,
           w1, b1, w2, b2, norm_f_g):
    b, s, d = x.shape
    n = b * s
    depth = norm1_g.shape[0]
    scale = HEAD_DIM ** -0.5 * LOG2E
    fw, sw = FOX_WIDTH, SB_WIDTH
    n_blocks = -(-n * TOP_K // EXPERT_TILE) + N_EXPERTS
    n_rows = n_blocks * EXPERT_TILE
    assert depth == 1, "the combine kernel fuses the final norm into the only layer"
    assert d == SUBLANES * LANES, "row-tile layout holds one model row per (8, 128) tile"
    wl = w_in[0]
    f0 = 3 * fw
    s0 = f0 + N_FOX_HEADS
    w_qk = jnp.concatenate(
        [wl[:, :fw] * scale, wl[:, s0:s0 + sw] * scale,
         wl[:, fw:2 * fw], wl[:, s0 + sw:s0 + 2 * sw]], axis=1).astype(BF16)
    w_vt = jnp.concatenate([wl[:, 2 * fw:f0], wl[:, s0 + 2 * sw:]], axis=1).T.astype(BF16)
    w_f = jnp.pad(wl[:, f0:s0], ((0, 0), (0, LANES - N_FOX_HEADS))).astype(BF16)
    bias_f = jnp.pad(b_f[0], (0, LANES - N_FOX_HEADS))[None, :]
    q, kf, ks, aug, vt = _in_proj(x, norm1_g[0][None, :], w_qk, w_vt, w_f, bias_f,
                                  _aug_selector())
    fox = _fox(q, kf, aug, vt)
    sb = _stick(q, ks, vt)
    x1, h2, e_k, w_k, rank_k, counts = _out_proj(
        x.reshape(n, d), fox.reshape(n, fw), sb.reshape(n, sw),
        g_fox[0][None, :], g_sb[0][None, :], w_out[0].astype(BF16),
        norm2_g[0][None, :], w_router[0].T, b_router[0][:, None])
    counts = counts[:, 0].astype(jnp.int32)
    dest, start = _route(counts, e_k, rank_k)
    dest_flat = dest.reshape(-1)
    xb = _dispatch(dest_flat, start, counts, h2, n_rows)
    yb = _experts(start, counts, xb, w1[0], b1[0], w2[0], b2[0])
    out = _combine(dest_flat, x1, w_k.T, norm_f_g[None, :], yb)
    return out.reshape(b, s, d)
```

```python
import functools

import jax
import jax.numpy as jnp
import numpy as np
from jax import lax
from jax.experimental import pallas as pl
from jax.experimental.pallas import tpu as pltpu

HEAD_DIM = 64
N_FOX_HEADS = 8
N_SB_HEADS = 8
FOX_WIDTH = N_FOX_HEADS * HEAD_DIM
SB_WIDTH = N_SB_HEADS * HEAD_DIM
N_EXPERTS = 32
TOP_K = 4
SWIGLU_LIMIT = 7.0
SWIGLU_ALPHA = 1.702
RMS_EPS = 1e-5
LOG2E = 1.4426950408889634

LANES = 128
SUBLANES = 8
PAIRS = FOX_WIDTH // LANES

ROW_TILE = 256
OUT_TILE = 1024
Q_TILE = 512
CUM_CHUNK = 256
KEY_CHUNK = 64
EXPERT_TILE = 256
DISPATCH_TILE = 1024
COMBINE_TILE = 512
VMEM_LIMIT = 56 * 1024 * 1024

F32 = jnp.float32
BF16 = jnp.bfloat16


def _dot(a, b):
    return jnp.dot(a, b, preferred_element_type=F32)


def _dot_nt(a, b):
    return lax.dot_general(a, b, (((1,), (1,)), ((), ())), preferred_element_type=F32)


def _split2(v):
    hi = v.astype(BF16)
    lo = (v - hi.astype(F32)).astype(BF16)
    return hi, lo


def _split3(v):
    hi = v.astype(BF16)
    r = v - hi.astype(F32)
    mid = r.astype(BF16)
    lo = (r - mid.astype(F32)).astype(BF16)
    return hi, mid, lo


def _softplus(z):
    return jnp.maximum(z, 0.0) + jnp.log1p(jnp.exp(-jnp.abs(z)))


def _softplus2(z2):
    sign_bit = jnp.uint32(0x80000000)
    neg_abs = lax.bitcast_convert_type(lax.bitcast_convert_type(z2, jnp.uint32) | sign_bit, F32)
    return jnp.maximum(z2, 0.0) + jnp.log2(1.0 + jnp.exp2(neg_abs))


def _load_row_tiles(ref, rows, first=0):
    return jnp.concatenate(
        [ref[pl.ds(first + c, rows, stride=SUBLANES), :] for c in range(SUBLANES)], axis=1)


def _store_row_tiles(ref, value):
    rows = value.shape[0]
    for c in range(SUBLANES):
        ref[pl.ds(c, rows, stride=SUBLANES), :] = value[:, c * LANES:(c + 1) * LANES]


def _store_packed_rows(ref, value):
    bits = lax.bitcast_convert_type(value.astype(BF16).astype(F32), jnp.uint32)
    for c in range(ref.shape[1]):
        lo = bits[:, 2 * c * LANES:(2 * c + 1) * LANES] >> 16
        hi = bits[:, (2 * c + 1) * LANES:(2 * c + 2) * LANES] & jnp.uint32(0xFFFF0000)
        ref[:, c, :] = lo | hi


def _load_packed_rows(ref):
    tiles = []
    for c in range(ref.shape[1]):
        word = ref[:, c, :]
        tiles.append(lax.bitcast_convert_type(word << 16, F32))
        tiles.append(lax.bitcast_convert_type(word & jnp.uint32(0xFFFF0000), F32))
    return jnp.concatenate(tiles, axis=1).astype(BF16)


def _rms(v, g):
    return v * lax.rsqrt(jnp.mean(v * v, axis=-1, keepdims=True) + RMS_EPS) * g


def _in_proj_kernel(x_ref, g_ref, wqk_ref, wvt_ref, wf_ref, bf_ref, sel_ref,
                    q_ref, kf_ref, ks_ref, aug_ref, vt_ref, carry_ref):
    @pl.when(pl.program_id(1) == 0)
    def _():
        carry_ref[...] = jnp.zeros_like(carry_ref)

    tm = x_ref.shape[1]
    mix = FOX_WIDTH + SB_WIDTH
    hb = _rms(x_ref[0], g_ref[...]).astype(BF16)
    qk = _dot(hb, wqk_ref[...])
    q_ref[0] = qk[:, :mix].astype(BF16)
    kf_ref[0] = qk[:, mix:mix + FOX_WIDTH].astype(BF16)
    ks_ref[0] = qk[:, mix + FOX_WIDTH:].astype(BF16)
    vt_ref[0] = _dot_nt(wvt_ref[...], hb).astype(BF16)
    log_f = -LOG2E * _softplus(-(_dot(hb, wf_ref[...]) + bf_ref[...]))
    row = lax.broadcasted_iota(jnp.int32, (tm, tm), 0)
    col = lax.broadcasted_iota(jnp.int32, (tm, tm), 1)
    tri = (col <= row).astype(BF16)
    hi, mid, lo = _split3(log_f)
    c = _dot(tri, hi) + _dot(tri, mid) + _dot(tri, lo) + carry_ref[...]
    carry_ref[...] = c[tm - 1:tm, :]
    c_hi, c_mid, c_lo = _split3(c)
    aug = _dot(c_hi, sel_ref[0]) + _dot(c_mid, sel_ref[1]) + _dot(c_lo, sel_ref[2])
    aug_ref[0] = aug.astype(BF16)


def _aug_selector():
    sel = np.zeros((3, LANES, FOX_WIDTH), np.float32)
    for t in range(3):
        for h in range(N_FOX_HEADS):
            sel[t, h, LANES * (h // 2) + 3 * (h % 2) + t] = -1.0
    return jnp.asarray(sel, BF16)


def _in_proj(x, g, wqk, wvt, wf, bf, sel):
    b, s, d = x.shape
    mix = FOX_WIDTH + SB_WIDTH
    tm = min(ROW_TILE, s)
    const2 = lambda i, j: (0, 0)
    return pl.pallas_call(
        _in_proj_kernel,
        name="in_proj",
        grid=(b, s // tm),
        in_specs=[
            pl.BlockSpec((1, tm, d), lambda i, j: (i, j, 0)),
            pl.BlockSpec((1, d), const2),
            pl.BlockSpec((d, 2 * mix), const2),
            pl.BlockSpec((mix, d), const2),
            pl.BlockSpec((d, LANES), const2),
            pl.BlockSpec((1, LANES), const2),
            pl.BlockSpec((3, LANES, FOX_WIDTH), lambda i, j: (0, 0, 0)),
        ],
        out_specs=[
            pl.BlockSpec((1, tm, mix), lambda i, j: (i, j, 0)),
            pl.BlockSpec((1, tm, FOX_WIDTH), lambda i, j: (i, j, 0)),
            pl.BlockSpec((1, tm, SB_WIDTH), lambda i, j: (i, j, 0)),
            pl.BlockSpec((1, tm, FOX_WIDTH), lambda i, j: (i, j, 0)),
            pl.BlockSpec((1, mix, tm), lambda i, j: (i, 0, j)),
        ],
        out_shape=[
            jax.ShapeDtypeStruct((b, s, mix), BF16),
            jax.ShapeDtypeStruct((b, s, FOX_WIDTH), BF16),
            jax.ShapeDtypeStruct((b, s, SB_WIDTH), BF16),
            jax.ShapeDtypeStruct((b, s, FOX_WIDTH), BF16),
            jax.ShapeDtypeStruct((b, mix, s), BF16),
        ],
        scratch_shapes=[pltpu.VMEM((1, LANES), F32)],
        compiler_params=pltpu.CompilerParams(
            dimension_semantics=("parallel", "arbitrary"),
            vmem_limit_bytes=VMEM_LIMIT),
    )(x, g, wqk, wvt, wf, bf, sel)


def _stack_heads(q, ones_cols):
    tq = q.shape[0]
    lane = lax.broadcasted_iota(jnp.int32, (tq, LANES), 1)
    low = lane < HEAD_DIM
    qf = q.astype(F32)
    halves = [jnp.where(low, qf, 0.0), jnp.where(low, 0.0, qf)]
    if ones_cols:
        halves = [jnp.concatenate([halves[hh], jnp.where(
            (lane >= 3 * hh) & (lane < 3 * hh + 3), 1.0, 0.0)], axis=1) for hh in range(2)]
    return jnp.concatenate(halves, axis=0).astype(q.dtype)


def _unstack_heads(acc_t, tq):
    sub = lax.broadcasted_iota(jnp.int32, (LANES, tq), 0)
    return jnp.where(sub < HEAD_DIM, acc_t[:, :tq], acc_t[:, tq:]).T


def _fox_kernel(*refs, tq):
    n_q = refs[0].shape[1] // tq
    lax.fori_loop(0, n_q, lambda i, _: _fox_query_block(i, tq, *refs), 0)


def _fox_query_block(i, tq, q_ref, k_ref, aug_ref, vt_ref, o_ref,
                     q2_ref, m_ref, l_ref, acc_ref, s_ref, p_ref, a_ref):
    q_rows = pl.ds(pl.multiple_of(i * tq, tq), tq)
    q2_ref[...] = _stack_heads(q_ref[0, q_rows, :], True)
    m_ref[...] = jnp.full(m_ref.shape, -1e30, F32)
    l_ref[...] = jnp.zeros_like(l_ref)
    acc_ref[...] = jnp.zeros_like(acc_ref)

    def scores(j, buf):
        k0 = pl.multiple_of(j * tq, tq)
        ka = jnp.concatenate(
            [k_ref[0, pl.ds(k0, tq), :], aug_ref[0, pl.ds(k0, tq), :]], axis=1)
        s_ref[buf] = _dot_nt(ka, q2_ref[...])

    def softmax_update(buf, diagonal):
        if diagonal:
            key = lax.broadcasted_iota(jnp.int32, (tq, 2 * tq), 0)
            qry = lax.broadcasted_iota(jnp.int32, (tq, 2 * tq), 1)
            qry = jnp.where(qry >= tq, qry - tq, qry)
            s_ref[buf] = jnp.where(key <= qry, s_ref[buf], -jnp.inf)
        m_old = m_ref[...]
        m_new = jnp.maximum(m_old, jnp.max(s_ref[buf], axis=0, keepdims=True))
        alpha = jnp.exp2(m_old - m_new)
        m_ref[...] = m_new
        a_ref[buf] = alpha
        part = jnp.zeros((SUBLANES, 2 * tq), F32)
        m_rows = jnp.broadcast_to(m_new, (SUBLANES, 2 * tq))
        for r in range(0, tq, KEY_CHUNK):
            sc = s_ref[buf, r:r + KEY_CHUNK, :].reshape(
                KEY_CHUNK // SUBLANES, SUBLANES, 2 * tq)
            pr = jnp.exp2(sc - m_rows)
            part = part + jnp.sum(pr, axis=0)
            p_ref[buf, r:r + KEY_CHUNK, :] = pr.reshape(KEY_CHUNK, 2 * tq).astype(BF16)
        l_ref[...] = l_ref[...] * alpha + jnp.sum(part, axis=0, keepdims=True)

    def accumulate(j, buf):
        k0 = pl.multiple_of(j * tq, tq)
        acc_ref[...] = (acc_ref[...] * a_ref[buf]
                        + _dot(vt_ref[0, :, pl.ds(k0, tq)], p_ref[buf]))

    def step(j, buf):
        scores(j + 1, 1 - buf)
        softmax_update(buf, False)
        accumulate(jnp.maximum(j - 1, 0), 1 - buf)

    def last(buf):
        softmax_update(buf, True)
        accumulate(jnp.maximum(i - 1, 0), 1 - buf)
        accumulate(i, buf)

    scores(0, 0)
    p_ref[1] = jnp.zeros(p_ref.shape[1:], BF16)
    a_ref[1] = jnp.ones(a_ref.shape[1:], F32)

    def two_steps(t, _):
        step(2 * t, 0)
        step(2 * t + 1, 1)
        return 0

    lax.fori_loop(0, i // 2, two_steps, 0)

    @pl.when(i % 2 == 1)
    def _():
        step(i - 1, 0)
        last(1)

    @pl.when(i % 2 == 0)
    def _():
        last(0)

    o_ref[0, q_rows, :] = _unstack_heads(acc_ref[...] / l_ref[...], tq)
    return 0


def _fox(q, kf, aug, vt):
    b, s, _ = q.shape
    tq = min(Q_TILE, s)
    return pl.pallas_call(
        functools.partial(_fox_kernel, tq=tq),
        name="fox",
        grid=(b, PAIRS),
        in_specs=[
            pl.BlockSpec((1, s, LANES), lambda bi, p: (bi, 0, p)),
            pl.BlockSpec((1, s, LANES), lambda bi, p: (bi, 0, p)),
            pl.BlockSpec((1, s, LANES), lambda bi, p: (bi, 0, p)),
            pl.BlockSpec((1, LANES, s), lambda bi, p: (bi, p, 0)),
        ],
        out_specs=pl.BlockSpec((1, s, LANES), lambda bi, p: (bi, 0, p)),
        out_shape=jax.ShapeDtypeStruct((b, s, FOX_WIDTH), F32),
        scratch_shapes=[
            pltpu.VMEM((2 * tq, 2 * LANES), BF16),
            pltpu.VMEM((1, 2 * tq), F32),
            pltpu.VMEM((1, 2 * tq), F32),
            pltpu.VMEM((LANES, 2 * tq), F32),
            pltpu.VMEM((2, tq, 2 * tq), F32),
            pltpu.VMEM((2, tq, 2 * tq), BF16),
            pltpu.VMEM((2, 1, 2 * tq), F32),
        ],
        compiler_params=pltpu.CompilerParams(
            dimension_semantics=("parallel", "parallel"),
            vmem_limit_bytes=VMEM_LIMIT),
    )(q, kf, aug, vt)


def _stick_kernel(*refs, tq):
    n_q = refs[0].shape[1] // tq
    lax.fori_loop(0, n_q, lambda i, _: _stick_query_block(i, tq, *refs), 0)


def _stick_query_block(i, tq, q_ref, k_ref, vt_ref, o_ref,
                       q2_ref, tail_ref, acc_ref, z_ref, a_ref):
    tc = min(CUM_CHUNK, tq)
    q_rows = pl.ds(pl.multiple_of(i * tq, tq), tq)
    q2_ref[...] = _stack_heads(q_ref[0, q_rows, :], False)
    tail_ref[...] = jnp.zeros_like(tail_ref)
    acc_ref[...] = jnp.zeros_like(acc_ref)
    kr = lax.broadcasted_iota(jnp.int32, (tc, tc), 0)
    kc = lax.broadcasted_iota(jnp.int32, (tc, tc), 1)
    later = (kc > kr).astype(BF16)

    def scores(j, buf):
        k0 = pl.multiple_of(j * tq, tq)
        z_ref[buf] = _dot_nt(k_ref[0, pl.ds(k0, tq), :], q2_ref[...])

    def weights(buf, diagonal):
        tail = tail_ref[...]
        for ch in reversed(range(tq // tc)):
            z_c = z_ref[buf, ch * tc:(ch + 1) * tc, :]
            sp_c = _softplus2(z_c)
            if diagonal:
                key = lax.broadcasted_iota(jnp.int32, (tc, 2 * tq), 0) + ch * tc
                qry = lax.broadcasted_iota(jnp.int32, (tc, 2 * tq), 1)
                qry = jnp.where(qry >= tq, qry - tq, qry)
                strict = key < qry
                sp_m = jnp.where(strict, sp_c, 0.0)
            else:
                sp_m = sp_c
            suffix = _dot(later, sp_m.astype(BF16))
            a = jnp.exp2((z_c - sp_c) - (suffix + tail))
            if diagonal:
                a = jnp.where(strict, a, 0.0)
            a_ref[buf, ch * tc:(ch + 1) * tc, :] = a.astype(BF16)
            tail = tail + suffix[0:1, :] + sp_m[0:1, :]
        tail_ref[...] = tail

    def accumulate(j, buf):
        k0 = pl.multiple_of(j * tq, tq)
        acc_ref[...] = acc_ref[...] + _dot(vt_ref[0, :, pl.ds(k0, tq)], a_ref[buf])

    def step(n, buf, diagonal):
        j = i - n
        scores(j - 1, 1 - buf)
        weights(buf, diagonal)
        if not diagonal:
            accumulate(j + 1, 1 - buf)

    def last(buf, diagonal):
        weights(buf, diagonal)
        if not diagonal:
            accumulate(1, 1 - buf)
        accumulate(0, buf)

    scores(i, 0)

    @pl.when(i == 0)
    def _():
        last(0, True)

    @pl.when(i > 0)
    def _():
        step(0, 0, True)

        def two_steps(t, _):
            step(1 + 2 * t, 1, False)
            step(2 + 2 * t, 0, False)
            return 0

        lax.fori_loop(0, (i - 1) // 2, two_steps, 0)

        @pl.when(i % 2 == 0)
        def _():
            step(i - 1, 1, False)
            last(0, False)

        @pl.when(i % 2 == 1)
        def _():
            last(1, False)

    o_ref[0, q_rows, :] = _unstack_heads(acc_ref[...], tq)
    return 0


def _stick(q, ks, vt):
    b, s, _ = q.shape
    tq = min(Q_TILE, s)
    return pl.pallas_call(
        functools.partial(_stick_kernel, tq=tq),
        name="stick",
        grid=(b, PAIRS),
        in_specs=[
            pl.BlockSpec((1, s, LANES), lambda bi, p: (bi, 0, PAIRS + p)),
            pl.BlockSpec((1, s, LANES), lambda bi, p: (bi, 0, p)),
            pl.BlockSpec((1, LANES, s), lambda bi, p: (bi, PAIRS + p, 0)),
        ],
        out_specs=pl.BlockSpec((1, s, LANES), lambda bi, p: (bi, 0, p)),
        out_shape=jax.ShapeDtypeStruct((b, s, SB_WIDTH), F32),
        scratch_shapes=[
            pltpu.VMEM((2 * tq, LANES), BF16),
            pltpu.VMEM((1, 2 * tq), F32),
            pltpu.VMEM((LANES, 2 * tq), F32),
            pltpu.VMEM((2, tq, 2 * tq), F32),
            pltpu.VMEM((2, tq, 2 * tq), BF16),
        ],
        compiler_params=pltpu.CompilerParams(
            dimension_semantics=("parallel", "parallel"),
            vmem_limit_bytes=VMEM_LIMIT),
    )(q, ks, vt)


def _out_proj_kernel(x_ref, fox_ref, sb_ref, gfox_ref, gsb_ref, wo_ref, g2_ref,
                     wr_ref, br_ref,
                     x1_ref, h2_ref, e_ref, w_ref, rank_ref, cnt_ref, carry_ref):
    @pl.when(pl.program_id(0) == 0)
    def _():
        carry_ref[...] = jnp.zeros_like(carry_ref)

    tm = x_ref.shape[0]
    a = _rms(fox_ref[...], gfox_ref[...]).astype(BF16)
    bb = _rms(sb_ref[...], gsb_ref[...]).astype(BF16)
    x1 = x_ref[...] + _dot(a, wo_ref[0:FOX_WIDTH, :]) + _dot(bb, wo_ref[FOX_WIDTH:, :])
    x1_ref[...] = x1
    h2 = _rms(x1, g2_ref[...])
    _store_packed_rows(h2_ref, h2)
    h_hi, h_lo = _split2(h2)
    w_hi, w_lo = _split2(wr_ref[...])
    logits = (_dot_nt(w_hi, h_hi) + _dot_nt(w_hi, h_lo) + _dot_nt(w_lo, h_hi)
              + br_ref[...])
    e_iota = lax.broadcasted_iota(jnp.int32, (N_EXPERTS, tm), 0)
    work = logits
    vals, idxs, sels = [], [], []
    for _ in range(TOP_K):
        m = jnp.max(work, axis=0, keepdims=True)
        idx = jnp.min(jnp.where(work == m, e_iota, N_EXPERTS), axis=0, keepdims=True)
        sel = e_iota == idx
        work = jnp.where(sel, -jnp.inf, work)
        vals.append(m)
        idxs.append(idx)
        sels.append(sel)
    exps = [jnp.exp(v - vals[0]) for v in vals]
    denom = exps[0] + exps[1] + exps[2] + exps[3]
    e_ref[...] = jnp.concatenate(idxs, axis=0)
    w_ref[...] = jnp.concatenate([ex / denom for ex in exps], axis=0)
    chosen = jnp.where(sels[0] | sels[1] | sels[2] | sels[3], 1.0, 0.0)
    row = lax.broadcasted_iota(jnp.int32, (tm, tm), 0)
    col = lax.broadcasted_iota(jnp.int32, (tm, tm), 1)
    before = (row < col).astype(BF16)
    rank = _dot(chosen.astype(BF16), before) + carry_ref[...]
    rank_ref[...] = jnp.concatenate(
        [jnp.sum(jnp.where(sel, rank, 0.0), axis=0, keepdims=True) for sel in sels],
        axis=0).astype(jnp.int32)
    total = carry_ref[...] + jnp.sum(chosen, axis=1, keepdims=True)
    carry_ref[...] = total
    cnt_ref[...] = jnp.broadcast_to(total, cnt_ref.shape)


def _out_proj(x2, fox2, sb2, gfox, gsb, wo, g2, wr_t, br):
    n, d = x2.shape
    tm = min(OUT_TILE, n)
    const = lambda i: (0, 0)
    return pl.pallas_call(
        _out_proj_kernel,
        name="out_proj",
        grid=(n // tm,),
        in_specs=[
            pl.BlockSpec((tm, d), lambda i: (i, 0)),
            pl.BlockSpec((tm, FOX_WIDTH), lambda i: (i, 0)),
            pl.BlockSpec((tm, SB_WIDTH), lambda i: (i, 0)),
            pl.BlockSpec((1, FOX_WIDTH), const),
            pl.BlockSpec((1, SB_WIDTH), const),
            pl.BlockSpec((FOX_WIDTH + SB_WIDTH, d), const),
            pl.BlockSpec((1, d), const),
            pl.BlockSpec((N_EXPERTS, d), const),
            pl.BlockSpec((N_EXPERTS, 1), const),
        ],
        out_specs=[
            pl.BlockSpec((tm, d), lambda i: (i, 0)),
            pl.BlockSpec((tm, d // (2 * LANES), LANES), lambda i: (i, 0, 0)),
            pl.BlockSpec((TOP_K, tm), lambda i: (0, i)),
            pl.BlockSpec((TOP_K, tm), lambda i: (0, i)),
            pl.BlockSpec((TOP_K, tm), lambda i: (0, i)),
            pl.BlockSpec((N_EXPERTS, LANES), const),
        ],
        out_shape=[
            jax.ShapeDtypeStruct((n, d), F32),
            jax.ShapeDtypeStruct((n, d // (2 * LANES), LANES), jnp.uint32),
            jax.ShapeDtypeStruct((TOP_K, n), jnp.int32),
            jax.ShapeDtypeStruct((TOP_K, n), F32),
            jax.ShapeDtypeStruct((TOP_K, n), jnp.int32),
            jax.ShapeDtypeStruct((N_EXPERTS, LANES), F32),
        ],
        scratch_shapes=[pltpu.VMEM((N_EXPERTS, 1), F32)],
        compiler_params=pltpu.CompilerParams(
            dimension_semantics=("arbitrary",),
            vmem_limit_bytes=VMEM_LIMIT),
    )(x2, fox2, sb2, gfox, gsb, wo, g2, wr_t, br)


def _route_kernel(cnt_ref, e_ref, rank_ref, dest_ref, start_ref):
    def per_expert(e, pstart):
        start_ref[e] = pstart
        return pstart + (cnt_ref[e] + EXPERT_TILE - 1) // EXPERT_TILE * EXPERT_TILE

    lax.fori_loop(0, N_EXPERTS, per_expert, 0)
    ev = e_ref[...]
    offs = jnp.zeros(ev.shape, jnp.int32)
    for e in range(N_EXPERTS):
        offs = jnp.where(ev == e, start_ref[e], offs)
    dest_ref[...] = rank_ref[...] + offs


def _route(counts, e_k, rank_k):
    smem = pl.BlockSpec(memory_space=pltpu.SMEM)
    vmem = pl.BlockSpec(memory_space=pltpu.VMEM)
    return pl.pallas_call(
        _route_kernel,
        name="route",
        in_specs=[smem, vmem, vmem],
        out_specs=[vmem, smem],
        out_shape=[
            jax.ShapeDtypeStruct(e_k.shape, jnp.int32),
            jax.ShapeDtypeStruct((N_EXPERTS,), jnp.int32),
        ],
    )(counts, e_k, rank_k)


def _zero_unused_blocks(dst_ref, zero_block_ref, sem, start_ref, cnt_ref):
    block_rows = zero_block_ref.shape[0]
    per_row = block_rows // EXPERT_TILE
    last = N_EXPERTS - 1
    used = (start_ref[last] + (cnt_ref[last] + EXPERT_TILE - 1) // EXPERT_TILE
            * EXPERT_TILE) * per_row
    count = (dst_ref.shape[0] - used) // block_rows

    def copy(b):
        rows = pl.ds(pl.multiple_of(used + b * block_rows, block_rows), block_rows)
        return pltpu.make_async_copy(zero_block_ref, dst_ref.at[rows], sem)

    def start(b, _):
        copy(b).start()
        return 0

    def wait(b, _):
        copy(b).wait()
        return 0

    lax.fori_loop(0, count, start, 0)
    lax.fori_loop(0, count, wait, 0)


def _dispatch_kernel(dest_ref, start_ref, cnt_ref, h_ref, xb_ref, zero_ref, sem, pad_sem,
                     *, n_tok, tile):
    base = pl.program_id(0) * tile

    @pl.when(pl.program_id(0) == 0)
    def _():
        zero_ref[...] = jnp.zeros_like(zero_ref)

        def pad_rows(e, wait):
            n_pad = (-cnt_ref[e]) % EXPERT_TILE
            pos = start_ref[e] + cnt_ref[e]
            size = EXPERT_TILE // 2
            while size >= 1:
                @pl.when((n_pad & size) != 0)
                def _(pos=pos, size=size):
                    copy = pltpu.make_async_copy(
                        zero_ref.at[pl.ds(0, size)], xb_ref.at[pl.ds(pos, size)], pad_sem)
                    if wait:
                        copy.wait()
                    else:
                        copy.start()

                pos = pos + (n_pad & size)
                size //= 2
            return 0

        lax.fori_loop(0, N_EXPERTS, lambda e, _: pad_rows(e, False), 0)
        lax.fori_loop(0, N_EXPERTS, lambda e, _: pad_rows(e, True), 0)
        _zero_unused_blocks(xb_ref, zero_ref, pad_sem, start_ref, cnt_ref)

    def issue(t, _):
        for k in range(TOP_K):
            d = dest_ref[k * n_tok + base + t]
            pltpu.make_async_copy(h_ref.at[t], xb_ref.at[d], sem).start(
                priority=k % 2)
        return 0

    lax.fori_loop(0, tile, issue, 0, unroll=8)
    rows = xb_ref.at[pl.ds(0, TOP_K * tile)]
    pltpu.make_async_copy(rows, rows, sem).wait()


def _dispatch(dest_flat, start, counts, h2, n_rows):
    n, words, _ = h2.shape
    tile = min(DISPATCH_TILE, n)
    return pl.pallas_call(
        functools.partial(_dispatch_kernel, n_tok=n, tile=tile),
        name="dispatch",
        grid_spec=pltpu.PrefetchScalarGridSpec(
            num_scalar_prefetch=3,
            grid=(n // tile,),
            in_specs=[pl.BlockSpec((tile, words, LANES), lambda i, dr, st, ct: (i, 0, 0))],
            out_specs=pl.BlockSpec(memory_space=pl.ANY),
            scratch_shapes=[pltpu.VMEM((EXPERT_TILE, words, LANES), h2.dtype),
                            pltpu.SemaphoreType.DMA(()),
                            pltpu.SemaphoreType.DMA(())],
        ),
        out_shape=jax.ShapeDtypeStruct((n_rows, words, LANES), h2.dtype),
        compiler_params=pltpu.CompilerParams(dimension_semantics=("arbitrary",)),
    )(dest_flat, start, counts, h2)


def _experts_kernel(start_ref, cnt_ref, xb_ref, w1_ref, b1_ref, w2_ref, b2_ref, y_ref,
                    w1b_ref, w2b_ref, xbuf_ref, ybuf_ref, sem_in, sem_out):
    e = pl.program_id(0)
    d_ff = w2_ref.shape[1]
    block_rows = EXPERT_TILE * SUBLANES
    n_blk = (cnt_ref[e] + EXPERT_TILE - 1) // EXPERT_TILE
    first = start_ref[e] * SUBLANES

    def rows(j):
        return pl.ds(pl.multiple_of(first + j * block_rows, block_rows), block_rows)

    def copy_in(j, slot):
        packed = pl.ds(pl.multiple_of(start_ref[e] + j * EXPERT_TILE, EXPERT_TILE), EXPERT_TILE)
        return pltpu.make_async_copy(xb_ref.at[packed], xbuf_ref.at[slot], sem_in.at[slot])

    def copy_out(j, slot):
        return pltpu.make_async_copy(ybuf_ref.at[slot], y_ref.at[rows(j), :], sem_out.at[slot])

    @pl.when(n_blk > 0)
    def _():
        copy_in(0, 0).start(priority=1)

    w1b_ref[...] = w1_ref[0].astype(BF16)
    w2b_ref[...] = w2_ref[0].astype(BF16)

    def block(j, _):
        slot = j % 2

        @pl.when(j + 1 < n_blk)
        def _():
            copy_in(j + 1, 1 - slot).start(priority=1)

        copy_in(j, slot).wait()

        @pl.when(j >= 2)
        def _():
            copy_out(j - 2, slot).wait()

        hh = _dot(_load_packed_rows(xbuf_ref.at[slot]), w1b_ref[...]) + b1_ref[0]
        glu = jnp.minimum(hh[:, :d_ff], SWIGLU_LIMIT)
        lin = jnp.clip(hh[:, d_ff:], -SWIGLU_LIMIT, SWIGLU_LIMIT)
        act = glu * (1.0 / (1.0 + jnp.exp(-SWIGLU_ALPHA * glu))) * (lin + 1.0)
        _store_row_tiles(ybuf_ref.at[slot], _dot(act.astype(BF16), w2b_ref[...]) + b2_ref[0])
        copy_out(j, slot).start(priority=1)
        return 0

    lax.fori_loop(0, n_blk, block, 0)

    for back in (1, 2):
        @pl.when(n_blk >= back)
        def _():
            copy_out(n_blk - back, (n_blk - back) % 2).wait()

    @pl.when(e == pl.num_programs(0) - 1)
    def _():
        ybuf_ref[0] = jnp.zeros(ybuf_ref.shape[1:], F32)
        _zero_unused_blocks(y_ref, ybuf_ref.at[0], sem_out.at[0], start_ref, cnt_ref)


def _experts(start, counts, xb, w1, b1, w2, b2):
    n_e, d, two_f = w1.shape
    n_rows, words, _ = xb.shape
    d_ff = two_f // 2
    block_rows = EXPERT_TILE * SUBLANES
    any_spec = pl.BlockSpec(memory_space=pl.ANY)
    return pl.pallas_call(
        _experts_kernel,
        name="experts",
        grid_spec=pltpu.PrefetchScalarGridSpec(
            num_scalar_prefetch=2,
            grid=(n_e,),
            in_specs=[
                any_spec,
                pl.BlockSpec((1, d, two_f), lambda e, st, ct: (e, 0, 0)),
                pl.BlockSpec((1, 1, two_f), lambda e, st, ct: (e, 0, 0)),
                pl.BlockSpec((1, d_ff, d), lambda e, st, ct: (e, 0, 0)),
                pl.BlockSpec((1, 1, d), lambda e, st, ct: (e, 0, 0)),
            ],
            out_specs=any_spec,
            scratch_shapes=[
                pltpu.VMEM((d, two_f), BF16),
                pltpu.VMEM((d_ff, d), BF16),
                pltpu.VMEM((2, EXPERT_TILE, words, LANES), xb.dtype),
                pltpu.VMEM((2, block_rows, LANES), F32),
                pltpu.SemaphoreType.DMA((2,)),
                pltpu.SemaphoreType.DMA((2,)),
            ],
        ),
        out_shape=jax.ShapeDtypeStruct((n_rows * SUBLANES, LANES), F32),
        compiler_params=pltpu.CompilerParams(
            dimension_semantics=("arbitrary",),
            vmem_limit_bytes=VMEM_LIMIT),
    )(start, counts, xb, w1, b1.reshape(n_e, 1, two_f), w2, b2.reshape(n_e, 1, d))


def _combine_kernel(dest_ref, x1_ref, w_ref, gf_ref, yb_ref, out_ref, buf_ref, sem,
                    *, n_tok):
    tile = x1_ref.shape[0]
    step = pl.program_id(0)
    slot = step % 2

    def gather(to_step, to_slot):
        def issue(t, _):
            for k in range(TOP_K):
                d = dest_ref[k * n_tok + to_step * tile + t]
                pltpu.make_async_copy(
                    yb_ref.at[pl.ds(pl.multiple_of(d * SUBLANES, SUBLANES), SUBLANES), :],
                    buf_ref.at[to_slot, pl.ds(pl.multiple_of((k * tile + t) * SUBLANES,
                                                              SUBLANES), SUBLANES), :],
                    sem.at[to_slot]).start(priority=k % 2)
            return 0

        lax.fori_loop(0, tile, issue, 0, unroll=8)

    @pl.when(step == 0)
    def _():
        gather(0, 0)

    @pl.when(step + 1 < pl.num_programs(0))
    def _():
        gather(step + 1, 1 - slot)

    pltpu.make_async_copy(yb_ref.at[pl.ds(0, TOP_K * tile * SUBLANES), :],
                          buf_ref.at[slot], sem.at[slot]).wait()
    acc = x1_ref[...]
    w = w_ref[...]
    rows = buf_ref.at[slot]
    for k in range(TOP_K):
        acc = acc + _load_row_tiles(rows, tile, k * tile * SUBLANES) * w[:, k:k + 1]
    out_ref[...] = _rms(acc, gf_ref[...])


def _combine(dest_flat, x1, w_tok, gf, yb):
    n, d = x1.shape
    tile = min(COMBINE_TILE, n)
    return pl.pallas_call(
        functools.partial(_combine_kernel, n_tok=n),
        name="combine",
        grid_spec=pltpu.PrefetchScalarGridSpec(
            num_scalar_prefetch=1,
            grid=(n // tile,),
            in_specs=[
                pl.BlockSpec((tile, d), lambda i, dr: (i, 0)),
                pl.BlockSpec((tile, TOP_K), lambda i, dr: (i, 0)),
                pl.BlockSpec((1, d), lambda i, dr: (0, 0)),
                pl.BlockSpec(memory_space=pl.ANY),
            ],
            out_specs=pl.BlockSpec((tile, d), lambda i, dr: (i, 0)),
            scratch_shapes=[pltpu.VMEM((2, TOP_K * tile * SUBLANES, LANES), F32),
                            pltpu.SemaphoreType.DMA((2,))],
        ),
        out_shape=jax.ShapeDtypeStruct((n, d), F32),
        compiler_params=pltpu.CompilerParams(
            dimension_semantics=("arbitrary",),
            vmem_limit_bytes=VMEM_LIMIT),
    )(dest_flat, x1, w_tok, gf, yb)


def kernel(x, norm1_g, w_in, b_f, g_fox, g_sb, w_out, norm2_g, w_router, b_router,
           w1, b1, w2, b2, norm_f_g):
    b, s, d = x.shape
    n = b * s
    depth = norm1_g.shape[0]
    scale = HEAD_DIM ** -0.5 * LOG2E
    fw, sw = FOX_WIDTH, SB_WIDTH
    n_blocks = -(-n * TOP_K // EXPERT_TILE) + N_EXPERTS
    n_rows = n_blocks * EXPERT_TILE
    assert depth == 1, "the combine kernel fuses the final norm into the only layer"
    assert d == SUBLANES * LANES, "row-tile layout holds one model row per (8, 128) tile"
    wl = w_in[0]
    f0 = 3 * fw
    s0 = f0 + N_FOX_HEADS
    w_qk = jnp.concatenate(
        [wl[:, :fw] * scale, wl[:, s0:s0 + sw] * scale,
         wl[:, fw:2 * fw], wl[:, s0 + sw:s0 + 2 * sw]], axis=1).astype(BF16)
    w_vt = jnp.concatenate([wl[:, 2 * fw:f0], wl[:, s0 + 2 * sw:]], axis=1).T.astype(BF16)
    w_f = jnp.pad(wl[:, f0:s0], ((0, 0), (0, LANES - N_FOX_HEADS))).astype(BF16)
    bias_f = jnp.pad(b_f[0], (0, LANES - N_FOX_HEADS))[None, :]
    q, kf, ks, aug, vt = _in_proj(x, norm1_g[0][None, :], w_qk, w_vt, w_f, bias_f,
                                  _aug_selector())
    fox = _fox(q, kf, aug, vt)
    sb = _stick(q, ks, vt)
    x1, h2, e_k, w_k, rank_k, counts = _out_proj(
        x.reshape(n, d), fox.reshape(n, fw), sb.reshape(n, sw),
        g_fox[0][None, :], g_sb[0][None, :], w_out[0].astype(BF16),
        norm2_g[0][None, :], w_router[0].T, b_router[0][:, None])
    counts = counts[:, 0].astype(jnp.int32)
    dest, start = _route(counts, e_k, rank_k)
    dest_flat = dest.reshape(-1)
    xb = _dispatch(dest_flat, start, counts, h2, n_rows)
    yb = _experts(start, counts, xb, w1[0], b1[0], w2[0], b2[0])
    out = _combine(dest_flat, x1, w_k.T, norm_f_g[None, :], yb)
    return out.reshape(b, s, d)
```

```python
import functools

import jax
import jax.numpy as jnp
import numpy as np
from jax import lax
from jax.experimental import pallas as pl
from jax.experimental.pallas import tpu as pltpu

HEAD_DIM = 64
N_FOX_HEADS = 8
N_SB_HEADS = 8
FOX_WIDTH = N_FOX_HEADS * HEAD_DIM
SB_WIDTH = N_SB_HEADS * HEAD_DIM
N_EXPERTS = 32
TOP_K = 4
SWIGLU_LIMIT = 7.0
SWIGLU_ALPHA = 1.702
RMS_EPS = 1e-5
LOG2E = 1.4426950408889634

LANES = 128
SUBLANES = 8
PAIRS = FOX_WIDTH // LANES

ROW_TILE = 256
OUT_TILE = 1024
Q_TILE = 512
CUM_CHUNK = 256
KEY_CHUNK = 64
EXPERT_TILE = 256
FF_CHUNK = 512
DISPATCH_TILE = 1024
COMBINE_TILE = 512
VMEM_LIMIT = 56 * 1024 * 1024

F32 = jnp.float32
BF16 = jnp.bfloat16


def _dot(a, b):
    return jnp.dot(a, b, preferred_element_type=F32)


def _dot_nt(a, b):
    return lax.dot_general(a, b, (((1,), (1,)), ((), ())), preferred_element_type=F32)


def _split2(v):
    hi = v.astype(BF16)
    lo = (v - hi.astype(F32)).astype(BF16)
    return hi, lo


def _split3(v):
    hi = v.astype(BF16)
    r = v - hi.astype(F32)
    mid = r.astype(BF16)
    lo = (r - mid.astype(F32)).astype(BF16)
    return hi, mid, lo


def _softplus(z):
    return jnp.maximum(z, 0.0) + jnp.log1p(jnp.exp(-jnp.abs(z)))


def _softplus2(z2):
    sign_bit = jnp.uint32(0x80000000)
    neg_abs = lax.bitcast_convert_type(lax.bitcast_convert_type(z2, jnp.uint32) | sign_bit, F32)
    return jnp.maximum(z2, 0.0) + jnp.log2(1.0 + jnp.exp2(neg_abs))


def _load_row_tiles(ref, rows, first=0):
    return jnp.concatenate(
        [ref[pl.ds(first + c, rows, stride=SUBLANES), :] for c in range(SUBLANES)], axis=1)


def _store_row_tiles(ref, value):
    rows = value.shape[0]
    for c in range(SUBLANES):
        ref[pl.ds(c, rows, stride=SUBLANES), :] = value[:, c * LANES:(c + 1) * LANES]


def _rms(v, g):
    return v * lax.rsqrt(jnp.mean(v * v, axis=-1, keepdims=True) + RMS_EPS) * g


def _in_proj_kernel(x_ref, g_ref, wqk_ref, wvt_ref, wf_ref, bf_ref, sel_ref,
                    q_ref, kf_ref, ks_ref, aug_ref, vt_ref, carry_ref):
    @pl.when(pl.program_id(1) == 0)
    def _():
        carry_ref[...] = jnp.zeros_like(carry_ref)

    tm = x_ref.shape[1]
    mix = FOX_WIDTH + SB_WIDTH
    hb = _rms(x_ref[0], g_ref[...]).astype(BF16)
    qk = _dot(hb, wqk_ref[...])
    q_ref[0] = qk[:, :mix].astype(BF16)
    kf_ref[0] = qk[:, mix:mix + FOX_WIDTH].astype(BF16)
    ks_ref[0] = qk[:, mix + FOX_WIDTH:].astype(BF16)
    vt_ref[0] = _dot_nt(wvt_ref[...], hb).astype(BF16)
    log_f = -LOG2E * _softplus(-(_dot(hb, wf_ref[...]) + bf_ref[...]))
    row = lax.broadcasted_iota(jnp.int32, (tm, tm), 0)
    col = lax.broadcasted_iota(jnp.int32, (tm, tm), 1)
    tri = (col <= row).astype(BF16)
    hi, mid, lo = _split3(log_f)
    c = _dot(tri, hi) + _dot(tri, mid) + _dot(tri, lo) + carry_ref[...]
    carry_ref[...] = c[tm - 1:tm, :]
    c_hi, c_mid, c_lo = _split3(c)
    aug = _dot(c_hi, sel_ref[0]) + _dot(c_mid, sel_ref[1]) + _dot(c_lo, sel_ref[2])
    aug_ref[0] = aug.astype(BF16)


def _aug_selector():
    sel = np.zeros((3, LANES, FOX_WIDTH), np.float32)
    for t in range(3):
        for h in range(N_FOX_HEADS):
            sel[t, h, LANES * (h // 2) + 3 * (h % 2) + t] = -1.0
    return jnp.asarray(sel, BF16)


def _in_proj(x, g, wqk, wvt, wf, bf, sel):
    b, s, d = x.shape
    mix = FOX_WIDTH + SB_WIDTH
    tm = min(ROW_TILE, s)
    const2 = lambda i, j: (0, 0)
    return pl.pallas_call(
        _in_proj_kernel,
        name="in_proj",
        grid=(b, s // tm),
        in_specs=[
            pl.BlockSpec((1, tm, d), lambda i, j: (i, j, 0)),
            pl.BlockSpec((1, d), const2),
            pl.BlockSpec((d, 2 * mix), const2),
            pl.BlockSpec((mix, d), const2),
            pl.BlockSpec((d, LANES), const2),
            pl.BlockSpec((1, LANES), const2),
            pl.BlockSpec((3, LANES, FOX_WIDTH), lambda i, j: (0, 0, 0)),
        ],
        out_specs=[
            pl.BlockSpec((1, tm, mix), lambda i, j: (i, j, 0)),
            pl.BlockSpec((1, tm, FOX_WIDTH), lambda i, j: (i, j, 0)),
            pl.BlockSpec((1, tm, SB_WIDTH), lambda i, j: (i, j, 0)),
            pl.BlockSpec((1, tm, FOX_WIDTH), lambda i, j: (i, j, 0)),
            pl.BlockSpec((1, mix, tm), lambda i, j: (i, 0, j)),
        ],
        out_shape=[
            jax.ShapeDtypeStruct((b, s, mix), BF16),
            jax.ShapeDtypeStruct((b, s, FOX_WIDTH), BF16),
            jax.ShapeDtypeStruct((b, s, SB_WIDTH), BF16),
            jax.ShapeDtypeStruct((b, s, FOX_WIDTH), BF16),
            jax.ShapeDtypeStruct((b, mix, s), BF16),
        ],
        scratch_shapes=[pltpu.VMEM((1, LANES), F32)],
        compiler_params=pltpu.CompilerParams(
            dimension_semantics=("parallel", "arbitrary"),
            vmem_limit_bytes=VMEM_LIMIT),
    )(x, g, wqk, wvt, wf, bf, sel)


def _stack_heads(q, ones_cols):
    tq = q.shape[0]
    lane = lax.broadcasted_iota(jnp.int32, (tq, LANES), 1)
    low = lane < HEAD_DIM
    qf = q.astype(F32)
    halves = [jnp.where(low, qf, 0.0), jnp.where(low, 0.0, qf)]
    if ones_cols:
        halves = [jnp.concatenate([halves[hh], jnp.where(
            (lane >= 3 * hh) & (lane < 3 * hh + 3), 1.0, 0.0)], axis=1) for hh in range(2)]
    return jnp.concatenate(halves, axis=0).astype(q.dtype)


def _unstack_heads(acc_t, tq):
    sub = lax.broadcasted_iota(jnp.int32, (LANES, tq), 0)
    return jnp.where(sub < HEAD_DIM, acc_t[:, :tq], acc_t[:, tq:]).T


def _fox_kernel(*refs, tq):
    n_q = refs[0].shape[1] // tq
    lax.fori_loop(0, n_q, lambda i, _: _fox_query_block(i, tq, *refs), 0)


def _fox_query_block(i, tq, q_ref, k_ref, aug_ref, vt_ref, o_ref,
                     q2_ref, m_ref, l_ref, acc_ref, s_ref, p_ref, a_ref):
    q_rows = pl.ds(pl.multiple_of(i * tq, tq), tq)
    q2_ref[...] = _stack_heads(q_ref[0, q_rows, :], True)
    m_ref[...] = jnp.full(m_ref.shape, -1e30, F32)
    l_ref[...] = jnp.zeros_like(l_ref)
    acc_ref[...] = jnp.zeros_like(acc_ref)

    def scores(j, buf):
        k0 = pl.multiple_of(j * tq, tq)
        ka = jnp.concatenate(
            [k_ref[0, pl.ds(k0, tq), :], aug_ref[0, pl.ds(k0, tq), :]], axis=1)
        s_ref[buf] = _dot_nt(ka, q2_ref[...])

    def softmax_update(buf, diagonal):
        if diagonal:
            key = lax.broadcasted_iota(jnp.int32, (tq, 2 * tq), 0)
            qry = lax.broadcasted_iota(jnp.int32, (tq, 2 * tq), 1)
            qry = jnp.where(qry >= tq, qry - tq, qry)
            s_ref[buf] = jnp.where(key <= qry, s_ref[buf], -jnp.inf)
        m_old = m_ref[...]
        m_new = jnp.maximum(m_old, jnp.max(s_ref[buf], axis=0, keepdims=True))
        alpha = jnp.exp2(m_old - m_new)
        m_ref[...] = m_new
        a_ref[buf] = alpha
        part = jnp.zeros((SUBLANES, 2 * tq), F32)
        m_rows = jnp.broadcast_to(m_new, (SUBLANES, 2 * tq))
        for r in range(0, tq, KEY_CHUNK):
            sc = s_ref[buf, r:r + KEY_CHUNK, :].reshape(
                KEY_CHUNK // SUBLANES, SUBLANES, 2 * tq)
            pr = jnp.exp2(sc - m_rows)
            part = part + jnp.sum(pr, axis=0)
            p_ref[buf, r:r + KEY_CHUNK, :] = pr.reshape(KEY_CHUNK, 2 * tq).astype(BF16)
        l_ref[...] = l_ref[...] * alpha + jnp.sum(part, axis=0, keepdims=True)

    def accumulate(j, buf):
        k0 = pl.multiple_of(j * tq, tq)
        acc_ref[...] = (acc_ref[...] * a_ref[buf]
                        + _dot(vt_ref[0, :, pl.ds(k0, tq)], p_ref[buf]))

    def step(j, buf):
        scores(j + 1, 1 - buf)
        softmax_update(buf, False)
        accumulate(jnp.maximum(j - 1, 0), 1 - buf)

    def last(buf):
        softmax_update(buf, True)
        accumulate(jnp.maximum(i - 1, 0), 1 - buf)
        accumulate(i, buf)

    scores(0, 0)
    p_ref[1] = jnp.zeros(p_ref.shape[1:], BF16)
    a_ref[1] = jnp.ones(a_ref.shape[1:], F32)

    def two_steps(t, _):
        step(2 * t, 0)
        step(2 * t + 1, 1)
        return 0

    lax.fori_loop(0, i // 2, two_steps, 0)

    @pl.when(i % 2 == 1)
    def _():
        step(i - 1, 0)
        last(1)

    @pl.when(i % 2 == 0)
    def _():
        last(0)

    o_ref[0, q_rows, :] = _unstack_heads(acc_ref[...] / l_ref[...], tq)
    return 0


def _fox(q, kf, aug, vt):
    b, s, _ = q.shape
    tq = min(Q_TILE, s)
    return pl.pallas_call(
        functools.partial(_fox_kernel, tq=tq),
        name="fox",
        grid=(b, PAIRS),
        in_specs=[
            pl.BlockSpec((1, s, LANES), lambda bi, p: (bi, 0, p)),
            pl.BlockSpec((1, s, LANES), lambda bi, p: (bi, 0, p)),
            pl.BlockSpec((1, s, LANES), lambda bi, p: (bi, 0, p)),
            pl.BlockSpec((1, LANES, s), lambda bi, p: (bi, p, 0)),
        ],
        out_specs=pl.BlockSpec((1, s, LANES), lambda bi, p: (bi, 0, p)),
        out_shape=jax.ShapeDtypeStruct((b, s, FOX_WIDTH), F32),
        scratch_shapes=[
            pltpu.VMEM((2 * tq, 2 * LANES), BF16),
            pltpu.VMEM((1, 2 * tq), F32),
            pltpu.VMEM((1, 2 * tq), F32),
            pltpu.VMEM((LANES, 2 * tq), F32),
            pltpu.VMEM((2, tq, 2 * tq), F32),
            pltpu.VMEM((2, tq, 2 * tq), BF16),
            pltpu.VMEM((2, 1, 2 * tq), F32),
        ],
        compiler_params=pltpu.CompilerParams(
            dimension_semantics=("parallel", "parallel"),
            vmem_limit_bytes=VMEM_LIMIT),
    )(q, kf, aug, vt)


def _stick_kernel(*refs, tq):
    n_q = refs[0].shape[1] // tq
    lax.fori_loop(0, n_q, lambda i, _: _stick_query_block(i, tq, *refs), 0)


def _stick_query_block(i, tq, q_ref, k_ref, vt_ref, o_ref,
                       q2_ref, tail_ref, acc_ref, z_ref, a_ref):
    tc = min(CUM_CHUNK, tq)
    q_rows = pl.ds(pl.multiple_of(i * tq, tq), tq)
    q2_ref[...] = _stack_heads(q_ref[0, q_rows, :], False)
    tail_ref[...] = jnp.zeros_like(tail_ref)
    acc_ref[...] = jnp.zeros_like(acc_ref)
    kr = lax.broadcasted_iota(jnp.int32, (tc, tc), 0)
    kc = lax.broadcasted_iota(jnp.int32, (tc, tc), 1)
    later = (kc > kr).astype(BF16)

    def scores(j, buf):
        k0 = pl.multiple_of(j * tq, tq)
        z_ref[buf] = _dot_nt(k_ref[0, pl.ds(k0, tq), :], q2_ref[...])

    def weights(buf, diagonal):
        tail = tail_ref[...]
        for ch in reversed(range(tq // tc)):
            z_c = z_ref[buf, ch * tc:(ch + 1) * tc, :]
            sp_c = _softplus2(z_c)
            if diagonal:
                key = lax.broadcasted_iota(jnp.int32, (tc, 2 * tq), 0) + ch * tc
                qry = lax.broadcasted_iota(jnp.int32, (tc, 2 * tq), 1)
                qry = jnp.where(qry >= tq, qry - tq, qry)
                strict = key < qry
                sp_m = jnp.where(strict, sp_c, 0.0)
            else:
                sp_m = sp_c
            suffix = _dot(later, sp_m.astype(BF16))
            a = jnp.exp2((z_c - sp_c) - (suffix + tail))
            if diagonal:
                a = jnp.where(strict, a, 0.0)
            a_ref[buf, ch * tc:(ch + 1) * tc, :] = a.astype(BF16)
            tail = tail + suffix[0:1, :] + sp_m[0:1, :]
        tail_ref[...] = tail

    def accumulate(j, buf):
        k0 = pl.multiple_of(j * tq, tq)
        acc_ref[...] = acc_ref[...] + _dot(vt_ref[0, :, pl.ds(k0, tq)], a_ref[buf])

    def step(n, buf, diagonal):
        j = i - n
        scores(j - 1, 1 - buf)
        weights(buf, diagonal)
        if not diagonal:
            accumulate(j + 1, 1 - buf)

    def last(buf, diagonal):
        weights(buf, diagonal)
        if not diagonal:
            accumulate(1, 1 - buf)
        accumulate(0, buf)

    scores(i, 0)

    @pl.when(i == 0)
    def _():
        last(0, True)

    @pl.when(i > 0)
    def _():
        step(0, 0, True)

        def two_steps(t, _):
            step(1 + 2 * t, 1, False)
            step(2 + 2 * t, 0, False)
            return 0

        lax.fori_loop(0, (i - 1) // 2, two_steps, 0)

        @pl.when(i % 2 == 0)
        def _():
            step(i - 1, 1, False)
            last(0, False)

        @pl.when(i % 2 == 1)
        def _():
            last(1, False)

    o_ref[0, q_rows, :] = _unstack_heads(acc_ref[...], tq)
    return 0


def _stick(q, ks, vt):
    b, s, _ = q.shape
    tq = min(Q_TILE, s)
    return pl.pallas_call(
        functools.partial(_stick_kernel, tq=tq),
        name="stick",
        grid=(b, PAIRS),
        in_specs=[
            pl.BlockSpec((1, s, LANES), lambda bi, p: (bi, 0, PAIRS + p)),
            pl.BlockSpec((1, s, LANES), lambda bi, p: (bi, 0, p)),
            pl.BlockSpec((1, LANES, s), lambda bi, p: (bi, PAIRS + p, 0)),
        ],
        out_specs=pl.BlockSpec((1, s, LANES), lambda bi, p: (bi, 0, p)),
        out_shape=jax.ShapeDtypeStruct((b, s, SB_WIDTH), F32),
        scratch_shapes=[
            pltpu.VMEM((2 * tq, LANES), BF16),
            pltpu.VMEM((1, 2 * tq), F32),
            pltpu.VMEM((LANES, 2 * tq), F32),
            pltpu.VMEM((2, tq, 2 * tq), F32),
            pltpu.VMEM((2, tq, 2 * tq), BF16),
        ],
        compiler_params=pltpu.CompilerParams(
            dimension_semantics=("parallel", "parallel"),
            vmem_limit_bytes=VMEM_LIMIT),
    )(q, ks, vt)


def _out_proj_kernel(x_ref, fox_ref, sb_ref, gfox_ref, gsb_ref, wo_ref, g2_ref,
                     wr_ref, br_ref,
                     x1_ref, h2_ref, e_ref, w_ref, rank_ref, cnt_ref, carry_ref):
    @pl.when(pl.program_id(0) == 0)
    def _():
        carry_ref[...] = jnp.zeros_like(carry_ref)

    tm = x_ref.shape[0]
    a = _rms(fox_ref[...], gfox_ref[...]).astype(BF16)
    bb = _rms(sb_ref[...], gsb_ref[...]).astype(BF16)
    x1 = x_ref[...] + _dot(a, wo_ref[0:FOX_WIDTH, :]) + _dot(bb, wo_ref[FOX_WIDTH:, :])
    x1_ref[...] = x1
    h2 = _rms(x1, g2_ref[...])
    _store_row_tiles(h2_ref, h2)
    h_hi, h_lo = _split2(h2)
    w_hi, w_lo = _split2(wr_ref[...])
    logits = (_dot_nt(w_hi, h_hi) + _dot_nt(w_hi, h_lo) + _dot_nt(w_lo, h_hi)
              + br_ref[...])
    e_iota = lax.broadcasted_iota(jnp.int32, (N_EXPERTS, tm), 0)
    work = logits
    vals, idxs, sels = [], [], []
    for _ in range(TOP_K):
        m = jnp.max(work, axis=0, keepdims=True)
        idx = jnp.min(jnp.where(work == m, e_iota, N_EXPERTS), axis=0, keepdims=True)
        sel = e_iota == idx
        work = jnp.where(sel, -jnp.inf, work)
        vals.append(m)
        idxs.append(idx)
        sels.append(sel)
    exps = [jnp.exp(v - vals[0]) for v in vals]
    denom = exps[0] + exps[1] + exps[2] + exps[3]
    e_ref[...] = jnp.concatenate(idxs, axis=0)
    w_ref[...] = jnp.concatenate([ex / denom for ex in exps], axis=0)
    chosen = jnp.where(sels[0] | sels[1] | sels[2] | sels[3], 1.0, 0.0)
    row = lax.broadcasted_iota(jnp.int32, (tm, tm), 0)
    col = lax.broadcasted_iota(jnp.int32, (tm, tm), 1)
    before = (row < col).astype(BF16)
    rank = _dot(chosen.astype(BF16), before) + carry_ref[...]
    rank_ref[...] = jnp.concatenate(
        [jnp.sum(jnp.where(sel, rank, 0.0), axis=0, keepdims=True) for sel in sels],
        axis=0).astype(jnp.int32)
    total = carry_ref[...] + jnp.sum(chosen, axis=1, keepdims=True)
    carry_ref[...] = total
    cnt_ref[...] = jnp.broadcast_to(total, cnt_ref.shape)


def _out_proj(x2, fox2, sb2, gfox, gsb, wo, g2, wr_t, br):
    n, d = x2.shape
    tm = min(OUT_TILE, n)
    const = lambda i: (0, 0)
    return pl.pallas_call(
        _out_proj_kernel,
        name="out_proj",
        grid=(n // tm,),
        in_specs=[
            pl.BlockSpec((tm, d), lambda i: (i, 0)),
            pl.BlockSpec((tm, FOX_WIDTH), lambda i: (i, 0)),
            pl.BlockSpec((tm, SB_WIDTH), lambda i: (i, 0)),
            pl.BlockSpec((1, FOX_WIDTH), const),
            pl.BlockSpec((1, SB_WIDTH), const),
            pl.BlockSpec((FOX_WIDTH + SB_WIDTH, d), const),
            pl.BlockSpec((1, d), const),
            pl.BlockSpec((N_EXPERTS, d), const),
            pl.BlockSpec((N_EXPERTS, 1), const),
        ],
        out_specs=[
            pl.BlockSpec((tm, d), lambda i: (i, 0)),
            pl.BlockSpec((tm * SUBLANES, LANES), lambda i: (i, 0)),
            pl.BlockSpec((TOP_K, tm), lambda i: (0, i)),
            pl.BlockSpec((TOP_K, tm), lambda i: (0, i)),
            pl.BlockSpec((TOP_K, tm), lambda i: (0, i)),
            pl.BlockSpec((N_EXPERTS, LANES), const),
        ],
        out_shape=[
            jax.ShapeDtypeStruct((n, d), F32),
            jax.ShapeDtypeStruct((n * SUBLANES, LANES), F32),
            jax.ShapeDtypeStruct((TOP_K, n), jnp.int32),
            jax.ShapeDtypeStruct((TOP_K, n), F32),
            jax.ShapeDtypeStruct((TOP_K, n), jnp.int32),
            jax.ShapeDtypeStruct((N_EXPERTS, LANES), F32),
        ],
        scratch_shapes=[pltpu.VMEM((N_EXPERTS, 1), F32)],
        compiler_params=pltpu.CompilerParams(
            dimension_semantics=("arbitrary",),
            vmem_limit_bytes=VMEM_LIMIT),
    )(x2, fox2, sb2, gfox, gsb, wo, g2, wr_t, br)


def _route_kernel(cnt_ref, e_ref, rank_ref, dest_ref, start_ref):
    def per_expert(e, pstart):
        start_ref[e] = pstart
        return pstart + (cnt_ref[e] + EXPERT_TILE - 1) // EXPERT_TILE * EXPERT_TILE

    lax.fori_loop(0, N_EXPERTS, per_expert, 0)
    ev = e_ref[...]
    offs = jnp.zeros(ev.shape, jnp.int32)
    for e in range(N_EXPERTS):
        offs = jnp.where(ev == e, start_ref[e], offs)
    dest_ref[...] = rank_ref[...] + offs


def _route(counts, e_k, rank_k):
    smem = pl.BlockSpec(memory_space=pltpu.SMEM)
    vmem = pl.BlockSpec(memory_space=pltpu.VMEM)
    return pl.pallas_call(
        _route_kernel,
        name="route",
        in_specs=[smem, vmem, vmem],
        out_specs=[vmem, smem],
        out_shape=[
            jax.ShapeDtypeStruct(e_k.shape, jnp.int32),
            jax.ShapeDtypeStruct((N_EXPERTS,), jnp.int32),
        ],
    )(counts, e_k, rank_k)


def _zero_unused_blocks(dst_ref, zero_block_ref, sem, start_ref, cnt_ref):
    block_rows = zero_block_ref.shape[0]
    last = N_EXPERTS - 1
    used = (start_ref[last] + (cnt_ref[last] + EXPERT_TILE - 1) // EXPERT_TILE
            * EXPERT_TILE) * SUBLANES
    count = (dst_ref.shape[0] - used) // block_rows

    def copy(b):
        rows = pl.ds(pl.multiple_of(used + b * block_rows, block_rows), block_rows)
        return pltpu.make_async_copy(zero_block_ref, dst_ref.at[rows, :], sem)

    def start(b, _):
        copy(b).start()
        return 0

    def wait(b, _):
        copy(b).wait()
        return 0

    lax.fori_loop(0, count, start, 0)
    lax.fori_loop(0, count, wait, 0)


def _dispatch_kernel(dest_ref, start_ref, cnt_ref, h_ref, xb_ref, zero_ref, sem, pad_sem,
                     *, n_tok, tile):
    base = pl.program_id(0) * tile

    @pl.when(pl.program_id(0) == 0)
    def _():
        zero_ref[...] = jnp.zeros_like(zero_ref)

        def pad_rows(e, wait):
            n_pad = (-cnt_ref[e]) % EXPERT_TILE
            pos = start_ref[e] + cnt_ref[e]
            size = EXPERT_TILE // 2
            while size >= 1:
                @pl.when((n_pad & size) != 0)
                def _(pos=pos, size=size):
                    row = pl.multiple_of(pos * SUBLANES, SUBLANES)
                    copy = pltpu.make_async_copy(
                        zero_ref.at[pl.ds(0, size * SUBLANES), :],
                        xb_ref.at[pl.ds(row, size * SUBLANES), :], pad_sem)
                    if wait:
                        copy.wait()
                    else:
                        copy.start()

                pos = pos + (n_pad & size)
                size //= 2
            return 0

        lax.fori_loop(0, N_EXPERTS, lambda e, _: pad_rows(e, False), 0)
        lax.fori_loop(0, N_EXPERTS, lambda e, _: pad_rows(e, True), 0)
        _zero_unused_blocks(xb_ref, zero_ref, pad_sem, start_ref, cnt_ref)

    def issue(t, _):
        src = h_ref.at[pl.ds(pl.multiple_of(t * SUBLANES, SUBLANES), SUBLANES), :]
        for k in range(TOP_K):
            d = dest_ref[k * n_tok + base + t]
            pltpu.make_async_copy(
                src, xb_ref.at[pl.ds(pl.multiple_of(d * SUBLANES, SUBLANES), SUBLANES), :],
                sem).start(priority=k % 2)
        return 0

    lax.fori_loop(0, tile, issue, 0, unroll=8)
    rows = xb_ref.at[pl.ds(0, TOP_K * tile * SUBLANES), :]
    pltpu.make_async_copy(rows, rows, sem).wait()


def _dispatch(dest_flat, start, counts, h2, n_rows):
    n = h2.shape[0] // SUBLANES
    tile = min(DISPATCH_TILE, n)
    return pl.pallas_call(
        functools.partial(_dispatch_kernel, n_tok=n, tile=tile),
        name="dispatch",
        grid_spec=pltpu.PrefetchScalarGridSpec(
            num_scalar_prefetch=3,
            grid=(n // tile,),
            in_specs=[pl.BlockSpec((tile * SUBLANES, LANES), lambda i, dr, st, ct: (i, 0))],
            out_specs=pl.BlockSpec(memory_space=pl.ANY),
            scratch_shapes=[pltpu.VMEM((EXPERT_TILE * SUBLANES, LANES), F32),
                            pltpu.SemaphoreType.DMA(()),
                            pltpu.SemaphoreType.DMA(())],
        ),
        out_shape=jax.ShapeDtypeStruct((n_rows * SUBLANES, LANES), h2.dtype),
        compiler_params=pltpu.CompilerParams(dimension_semantics=("arbitrary",)),
    )(dest_flat, start, counts, h2)


def _experts_kernel(start_ref, cnt_ref, xb_ref, w1_ref, b1_ref, w2_ref, b2_ref, y_ref,
                    w1b_ref, w2b_ref, xbuf_ref, ybuf_ref, sem_in, sem_out):
    e = pl.program_id(0)
    d_ff = w2_ref.shape[1]
    block_rows = EXPERT_TILE * SUBLANES
    n_blk = (cnt_ref[e] + EXPERT_TILE - 1) // EXPERT_TILE
    first = start_ref[e] * SUBLANES

    def rows(j):
        return pl.ds(pl.multiple_of(first + j * block_rows, block_rows), block_rows)

    def copy_in(j, slot):
        return pltpu.make_async_copy(xb_ref.at[rows(j), :], xbuf_ref.at[slot], sem_in.at[slot])

    def copy_out(j, slot):
        return pltpu.make_async_copy(ybuf_ref.at[slot], y_ref.at[rows(j), :], sem_out.at[slot])

    @pl.when(n_blk > 0)
    def _():
        copy_in(0, 0).start(priority=1)

    w1b_ref[...] = w1_ref[0].astype(BF16)
    w2b_ref[...] = w2_ref[0].astype(BF16)

    def block(j, _):
        slot = j % 2

        @pl.when(j + 1 < n_blk)
        def _():
            copy_in(j + 1, 1 - slot).start(priority=1)

        copy_in(j, slot).wait()

        @pl.when(j >= 2)
        def _():
            copy_out(j - 2, slot).wait()

        x = _load_row_tiles(xbuf_ref.at[slot], EXPERT_TILE).astype(BF16)
        y = b2_ref[0]
        for c in range(0, d_ff, FF_CHUNK):
            glu = jnp.minimum(
                _dot(x, w1b_ref[:, c:c + FF_CHUNK]) + b1_ref[0, :, c:c + FF_CHUNK],
                SWIGLU_LIMIT)
            lin = jnp.clip(
                _dot(x, w1b_ref[:, d_ff + c:d_ff + c + FF_CHUNK])
                + b1_ref[0, :, d_ff + c:d_ff + c + FF_CHUNK], -SWIGLU_LIMIT, SWIGLU_LIMIT)
            act = glu * (1.0 / (1.0 + jnp.exp(-SWIGLU_ALPHA * glu))) * (lin + 1.0)
            y = y + _dot(act.astype(BF16), w2b_ref[c:c + FF_CHUNK, :])
        _store_row_tiles(ybuf_ref.at[slot], y)
        copy_out(j, slot).start(priority=1)
        return 0

    lax.fori_loop(0, n_blk, block, 0)

    for back in (1, 2):
        @pl.when(n_blk >= back)
        def _():
            copy_out(n_blk - back, (n_blk - back) % 2).wait()

    @pl.when(e == pl.num_programs(0) - 1)
    def _():
        ybuf_ref[0] = jnp.zeros(ybuf_ref.shape[1:], F32)
        _zero_unused_blocks(y_ref, ybuf_ref.at[0], sem_out.at[0], start_ref, cnt_ref)


def _experts(start, counts, xb, w1, b1, w2, b2):
    n_e, d, two_f = w1.shape
    d_ff = two_f // 2
    block_rows = EXPERT_TILE * SUBLANES
    any_spec = pl.BlockSpec(memory_space=pl.ANY)
    return pl.pallas_call(
        _experts_kernel,
        name="experts",
        grid_spec=pltpu.PrefetchScalarGridSpec(
            num_scalar_prefetch=2,
            grid=(n_e,),
            in_specs=[
                any_spec,
                pl.BlockSpec((1, d, two_f), lambda e, st, ct: (e, 0, 0)),
                pl.BlockSpec((1, 1, two_f), lambda e, st, ct: (e, 0, 0)),
                pl.BlockSpec((1, d_ff, d), lambda e, st, ct: (e, 0, 0)),
                pl.BlockSpec((1, 1, d), lambda e, st, ct: (e, 0, 0)),
            ],
            out_specs=any_spec,
            scratch_shapes=[
                pltpu.VMEM((d, two_f), BF16),
                pltpu.VMEM((d_ff, d), BF16),
                pltpu.VMEM((2, block_rows, LANES), F32),
                pltpu.VMEM((2, block_rows, LANES), F32),
                pltpu.SemaphoreType.DMA((2,)),
                pltpu.SemaphoreType.DMA((2,)),
            ],
        ),
        out_shape=jax.ShapeDtypeStruct(xb.shape, F32),
        compiler_params=pltpu.CompilerParams(
            dimension_semantics=("arbitrary",),
            vmem_limit_bytes=VMEM_LIMIT),
    )(start, counts, xb, w1, b1.reshape(n_e, 1, two_f), w2, b2.reshape(n_e, 1, d))


def _combine_kernel(dest_ref, x1_ref, w_ref, gf_ref, yb_ref, out_ref, buf_ref, sem,
                    *, n_tok):
    tile = x1_ref.shape[0]
    step = pl.program_id(0)
    slot = step % 2

    def gather(to_step, to_slot):
        def issue(t, _):
            for k in range(TOP_K):
                d = dest_ref[k * n_tok + to_step * tile + t]
                pltpu.make_async_copy(
                    yb_ref.at[pl.ds(pl.multiple_of(d * SUBLANES, SUBLANES), SUBLANES), :],
                    buf_ref.at[to_slot, pl.ds(pl.multiple_of((k * tile + t) * SUBLANES,
                                                              SUBLANES), SUBLANES), :],
                    sem.at[to_slot]).start(priority=k % 2)
            return 0

        lax.fori_loop(0, tile, issue, 0, unroll=8)

    @pl.when(step == 0)
    def _():
        gather(0, 0)

    @pl.when(step + 1 < pl.num_programs(0))
    def _():
        gather(step + 1, 1 - slot)

    pltpu.make_async_copy(yb_ref.at[pl.ds(0, TOP_K * tile * SUBLANES), :],
                          buf_ref.at[slot], sem.at[slot]).wait()
    acc = x1_ref[...]
    w = w_ref[...]
    rows = buf_ref.at[slot]
    for k in range(TOP_K):
        acc = acc + _load_row_tiles(rows, tile, k * tile * SUBLANES) * w[:, k:k + 1]
    out_ref[...] = _rms(acc, gf_ref[...])


def _combine(dest_flat, x1, w_tok, gf, yb):
    n, d = x1.shape
    tile = min(COMBINE_TILE, n)
    return pl.pallas_call(
        functools.partial(_combine_kernel, n_tok=n),
        name="combine",
        grid_spec=pltpu.PrefetchScalarGridSpec(
            num_scalar_prefetch=1,
            grid=(n // tile,),
            in_specs=[
                pl.BlockSpec((tile, d), lambda i, dr: (i, 0)),
                pl.BlockSpec((tile, TOP_K), lambda i, dr: (i, 0)),
                pl.BlockSpec((1, d), lambda i, dr: (0, 0)),
                pl.BlockSpec(memory_space=pl.ANY),
            ],
            out_specs=pl.BlockSpec((tile, d), lambda i, dr: (i, 0)),
            scratch_shapes=[pltpu.VMEM((2, TOP_K * tile * SUBLANES, LANES), F32),
                            pltpu.SemaphoreType.DMA((2,))],
        ),
        out_shape=jax.ShapeDtypeStruct((n, d), F32),
        compiler_params=pltpu.CompilerParams(
            dimension_semantics=("arbitrary",),
            vmem_limit_bytes=VMEM_LIMIT),
    )(dest_flat, x1, w_tok, gf, yb)


def kernel(x, norm1_g, w_in, b_f, g_fox, g_sb, w_out, norm2_g, w_router, b_router,
           w1, b1, w2, b2, norm_f_g):
    b, s, d = x.shape
    n = b * s
    depth = norm1_g.shape[0]
    scale = HEAD_DIM ** -0.5 * LOG2E
    fw, sw = FOX_WIDTH, SB_WIDTH
    n_blocks = -(-n * TOP_K // EXPERT_TILE) + N_EXPERTS
    n_rows = n_blocks * EXPERT_TILE
    assert depth == 1, "the combine kernel fuses the final norm into the only layer"
    assert d == SUBLANES * LANES, "row-tile layout holds one model row per (8, 128) tile"
    wl = w_in[0]
    f0 = 3 * fw
    s0 = f0 + N_FOX_HEADS
    w_qk = jnp.concatenate(
        [wl[:, :fw] * scale, wl[:, s0:s0 + sw] * scale,
         wl[:, fw:2 * fw], wl[:, s0 + sw:s0 + 2 * sw]], axis=1).astype(BF16)
    w_vt = jnp.concatenate([wl[:, 2 * fw:f0], wl[:, s0 + 2 * sw:]], axis=1).T.astype(BF16)
    w_f = jnp.pad(wl[:, f0:s0], ((0, 0), (0, LANES - N_FOX_HEADS))).astype(BF16)
    bias_f = jnp.pad(b_f[0], (0, LANES - N_FOX_HEADS))[None, :]
    q, kf, ks, aug, vt = _in_proj(x, norm1_g[0][None, :], w_qk, w_vt, w_f, bias_f,
                                  _aug_selector())
    fox = _fox(q, kf, aug, vt)
    sb = _stick(q, ks, vt)
    x1, h2, e_k, w_k, rank_k, counts = _out_proj(
        x.reshape(n, d), fox.reshape(n, fw), sb.reshape(n, sw),
        g_fox[0][None, :], g_sb[0][None, :], w_out[0].astype(BF16),
        norm2_g[0][None, :], w_router[0].T, b_router[0][:, None])
    counts = counts[:, 0].astype(jnp.int32)
    dest, start = _route(counts, e_k, rank_k)
    dest_flat = dest.reshape(-1)
    xb = _dispatch(dest_flat, start, counts, h2, n_rows)
    yb = _experts(start, counts, xb, w1[0], b1[0], w2[0], b2[0])
    out = _combine(dest_flat, x1, w_k.T, norm_f_g[None, :], yb)
    return out.reshape(b, s, d)
```
